```python
import math
import jax, jax.numpy as jnp
from jax import lax
import numpy as np

D_MODEL = 1024
BATCH = 16
SEQ = 4096
DEPTH = 4

CHUNK = 64
EPS = 1e-6
N_BRANCH = 3

SSD_D_INNER = D_MODEL
SSD_HEAD_DIM = 64
SSD_N_HEADS = SSD_D_INNER // SSD_HEAD_DIM
SSD_N_GROUPS = 2
SSD_HEADS_PER_GROUP = SSD_N_HEADS // SSD_N_GROUPS
SSD_D_STATE = 128
SSD_CONV = 4
SSD_CONV_DIM = SSD_D_INNER + 2 * SSD_N_GROUPS * SSD_D_STATE

ATT_HEAD_DIM = 64
ATT_N_HEADS = 8
ATT_N_KV = 2
ATT_REP = ATT_N_HEADS // ATT_N_KV
ATT_WINDOW = 128
ATT_WIN_CHUNKS = (ATT_WINDOW + CHUNK - 1) // CHUNK
ATT_BAND = (ATT_WIN_CHUNKS + 1) * CHUNK
N_BUCKETS = 32
MAX_DISTANCE = 128

POOL_WIDTH = D_MODEL // 2
POOL_WINDOWS = (2, 4, 8, 16)
POOL_GROUP = POOL_WIDTH // len(POOL_WINDOWS)

D_FF = 4 * D_MODEL

GATE_END = N_BRANCH * D_MODEL
Z_END = GATE_END + SSD_D_INNER
XBC_END = Z_END + SSD_CONV_DIM
DT_END = XBC_END + SSD_N_HEADS
Q_END = DT_END + ATT_N_HEADS * ATT_HEAD_DIM
K_END = Q_END + ATT_N_KV * ATT_HEAD_DIM
V_END = K_END + ATT_N_KV * ATT_HEAD_DIM
IN_COLS = V_END + POOL_WIDTH

kernel_name = "hybrid_ssd_swa_pool_trunk"


def rmsnorm(x, w):
    xf = x.astype(jnp.float32)
    y = xf * lax.rsqrt(jnp.mean(xf * xf, axis=-1, keepdims=True) + EPS)
    return (y * w.astype(jnp.float32)).astype(x.dtype)


def t5_buckets(rel):
    nb = N_BUCKETS // 2
    out = (rel > 0).astype(jnp.int32) * nb
    n = jnp.abs(rel)
    max_exact = nb // 2
    nf = jnp.maximum(n, 1).astype(jnp.float32)
    large = max_exact + (jnp.log(nf / max_exact) / math.log(MAX_DISTANCE / max_exact)
                         * (nb - max_exact)).astype(jnp.int32)
    large = jnp.minimum(large, nb - 1)
    return out + jnp.where(n < max_exact, n, large)


def ssd_mixer(z, xbc, dt_raw, conv_w, conv_b, dt_bias, a_log, d_skip, norm_w):
    b, s, _ = xbc.shape
    nc = s // CHUNK
    G, R, P, N = SSD_N_GROUPS, SSD_HEADS_PER_GROUP, SSD_HEAD_DIM, SSD_D_STATE
    xbc = lax.conv_general_dilated(
        xbc, conv_w[:, None, :].astype(xbc.dtype), window_strides=(1,),
        padding=[(SSD_CONV - 1, 0)], dimension_numbers=('NWC', 'WIO', 'NWC'),
        feature_group_count=SSD_CONV_DIM) + conv_b.astype(xbc.dtype)
    xbc = jax.nn.silu(xbc).astype(jnp.float32)
    xs, bm, cm = jnp.split(xbc, [SSD_D_INNER, SSD_D_INNER + G * N], axis=-1)
    xs = xs.reshape(b, nc, CHUNK, G, R, P)
    bm = bm.reshape(b, nc, CHUNK, G, N)
    cm = cm.reshape(b, nc, CHUNK, G, N)
    dt = jax.nn.softplus(dt_raw.astype(jnp.float32) + dt_bias.astype(jnp.float32))
    dt = dt.reshape(b, nc, CHUNK, G, R)
    a = -jnp.exp(a_log.astype(jnp.float32)).reshape(G, R)
    a_cs = jnp.cumsum(dt * a, axis=2)
    a_t = jnp.moveaxis(a_cs, 2, -1)
    xdt = xs * dt[..., None]
    seg = a_t[..., :, None] - a_t[..., None, :]
    causal = jnp.tril(jnp.ones((CHUNK, CHUNK), dtype=bool))
    lmat = jnp.exp(jnp.where(causal, seg, -jnp.inf))
    cb = jnp.einsum('bclgn,bcsgn->bcgls', cm, bm)
    y_diag = jnp.einsum('bcgls,bcgrls,bcsgrp->bclgrp', cb, lmat, xdt)
    decay_states = jnp.exp(a_t[..., -1:] - a_t)
    chunk_states = jnp.einsum('bclgn,bcgrl,bclgrp->bcgrpn', bm, decay_states, xdt)
    chunk_decay = jnp.exp(a_t[..., -1])

    def step(h, inp):
        st, dec = inp
        return h * dec[..., None, None] + st, h

    h0 = jnp.zeros((b, G, R, P, N), jnp.float32)
    _, prev = lax.scan(step, h0, (jnp.moveaxis(chunk_states, 1, 0),
                                  jnp.moveaxis(chunk_decay, 1, 0)))
    prev = jnp.moveaxis(prev, 0, 1)
    y_off = jnp.einsum('bclgn,bcgrpn,bcgrl->bclgrp', cm, prev, jnp.exp(a_t))
    y = y_diag + y_off + xs * d_skip.astype(jnp.float32).reshape(G, R)[..., None]
    y = y.reshape(b, s, SSD_D_INNER) * jax.nn.silu(z.astype(jnp.float32))
    yg = y.reshape(b, s, G, SSD_D_INNER // G)
    yg = yg * lax.rsqrt(jnp.mean(yg * yg, axis=-1, keepdims=True) + EPS)
    y = yg.reshape(b, s, SSD_D_INNER) * norm_w.astype(jnp.float32)
    return y.astype(z.dtype)


def swa_sink_attention(q, k, v, sinks, rel_bias):
    b, s, _ = q.shape
    nc = s // CHUNK
    G, R, Dh = ATT_N_KV, ATT_REP, ATT_HEAD_DIM
    pad = ATT_WIN_CHUNKS * CHUNK
    q = q.reshape(b, nc, CHUNK, G, R, Dh)

    def band(t):
        t = jnp.pad(t, ((0, 0), (pad, 0), (0, 0))).reshape(b, nc + ATT_WIN_CHUNKS, CHUNK, G, Dh)
        return jnp.concatenate([t[:, i:i + nc] for i in range(ATT_WIN_CHUNKS + 1)], axis=2)

    kb, vb = band(k), band(v)
    scores = jnp.einsum('bcqgrd,bckgd->bcgrqk', q, kb).astype(jnp.float32) * (Dh ** -0.5)
    qpos = jnp.arange(CHUNK, dtype=jnp.int32)
    kpos = jnp.arange(ATT_BAND, dtype=jnp.int32) - pad
    bucket = t5_buckets(kpos[None, :] - qpos[:, None])
    bias = jnp.moveaxis(rel_bias.astype(jnp.float32)[bucket], -1, 0).reshape(G, R, CHUNK, ATT_BAND)
    key_abs = jnp.arange(nc, dtype=jnp.int32)[:, None] * CHUNK + kpos[None, :]
    valid = (key_abs >= 0)[None, :, None, None, None, :]
    scores = jnp.where(valid, scores + bias, -jnp.inf)
    sink = jnp.broadcast_to(sinks.astype(jnp.float32).reshape(1, 1, G, R, 1, 1),
                            scores.shape[:-1] + (1,))
    probs = jax.nn.softmax(jnp.concatenate([scores, sink], axis=-1), axis=-1)[..., :-1]
    out = jnp.einsum('bcgrqk,bckgd->bcqgrd', probs.astype(vb.dtype), vb)
    return out.reshape(b, s, ATT_N_HEADS * Dh)


def pool_mixer(u, pool_w, pool_scale):
    b, s, _ = u.shape
    uf = u.astype(jnp.float32)
    cs = jnp.pad(jnp.cumsum(uf, axis=1), ((0, 0), (1, 0), (0, 0)))
    t = jnp.arange(s, dtype=jnp.int32)
    groups = []
    for gi, w in enumerate(POOL_WINDOWS):
        c = cs[:, :, gi * POOL_GROUP:(gi + 1) * POOL_GROUP]
        lagged = jnp.pad(c, ((0, 0), (w - 1, 0), (0, 0)))[:, :s]
        cnt = jnp.minimum(t + 1, w).astype(jnp.float32)
        groups.append((c[:, 1:] - lagged) / cnt[None, :, None]
                      - uf[:, :, gi * POOL_GROUP:(gi + 1) * POOL_GROUP])
    pooled = jnp.stack(groups, axis=2)
    mixed = jnp.einsum('bsgc,gcd->bsgd', pooled, pool_w.astype(jnp.float32))
    mixed = mixed.reshape(b, s, POOL_WIDTH) * pool_scale.astype(jnp.float32)
    return mixed.astype(u.dtype)


def setup_inputs(seed: int = 0) -> dict:
    key = jax.random.key(seed)
    ks = jax.random.split(key, 24)
    f32 = jnp.float32
    nrm = lambda k, shape, scale: jax.random.normal(k, shape, f32) * scale
    dt0 = jnp.exp(jax.random.uniform(ks[5], (DEPTH, SSD_N_HEADS), f32,
                                     math.log(1e-3), math.log(1e-1)))
    return {
        "x": nrm(ks[0], (BATCH, SEQ, D_MODEL), 1.0),
        "norm1_w": 1.0 + nrm(ks[1], (DEPTH, D_MODEL), 0.05),
        "w_in": nrm(ks[2], (DEPTH, D_MODEL, IN_COLS), D_MODEL ** -0.5),
        "ssd_conv_w": nrm(ks[3], (DEPTH, SSD_CONV, SSD_CONV_DIM), SSD_CONV ** -0.5),
        "ssd_conv_b": nrm(ks[4], (DEPTH, SSD_CONV_DIM), 0.01),
        "ssd_dt_bias": dt0 + jnp.log(-jnp.expm1(-dt0)),
        "ssd_a_log": jnp.log(jax.random.uniform(ks[6], (DEPTH, SSD_N_HEADS), f32, 1.0, 16.0)),
        "ssd_d": 1.0 + nrm(ks[7], (DEPTH, SSD_N_HEADS), 0.1),
        "ssd_norm_w": 1.0 + nrm(ks[8], (DEPTH, SSD_D_INNER), 0.05),
        "attn_sinks": nrm(ks[9], (DEPTH, ATT_N_HEADS), 0.5),
        "rel_bias": nrm(ks[10], (N_BUCKETS, ATT_N_HEADS), 0.5),
        "pool_w": nrm(ks[11], (DEPTH, len(POOL_WINDOWS), POOL_GROUP, POOL_GROUP), POOL_GROUP ** -0.5),
        "pool_scale": 1.0 + nrm(ks[12], (DEPTH, POOL_WIDTH), 0.1),
        "w_up_ssd": nrm(ks[13], (DEPTH, SSD_D_INNER, D_MODEL), SSD_D_INNER ** -0.5),
        "w_up_attn": nrm(ks[14], (DEPTH, ATT_N_HEADS * ATT_HEAD_DIM, D_MODEL), (ATT_N_HEADS * ATT_HEAD_DIM) ** -0.5),
        "w_up_pool": nrm(ks[15], (DEPTH, POOL_WIDTH, D_MODEL), POOL_WIDTH ** -0.5),
        "w_out": nrm(ks[16], (DEPTH, D_MODEL, D_MODEL), D_MODEL ** -0.5),
        "norm2_w": 1.0 + nrm(ks[17], (DEPTH, D_MODEL), 0.05),
        "w_mlp_in": nrm(ks[18], (DEPTH, D_MODEL, D_FF), D_MODEL ** -0.5),
        "w_mlp_out": nrm(ks[19], (DEPTH, D_FF, D_MODEL), D_FF ** -0.5),
        "final_norm_w": 1.0 + nrm(ks[20], (D_MODEL,), 0.05),
    }


def reference(x, norm1_w, w_in, ssd_conv_w, ssd_conv_b, ssd_dt_bias, ssd_a_log, ssd_d,
              ssd_norm_w, attn_sinks, rel_bias, pool_w, pool_scale, w_up_ssd, w_up_attn,
              w_up_pool, w_out, norm2_w, w_mlp_in, w_mlp_out, final_norm_w):
    for layer in range(DEPTH):
        h = rmsnorm(x, norm1_w[layer])
        proj = h @ w_in[layer]
        gates, z, xbc, dt_raw, q, k, v, u = jnp.split(
            proj, [GATE_END, Z_END, XBC_END, DT_END, Q_END, K_END, V_END], axis=-1)
        y_ssd = ssd_mixer(z, xbc, dt_raw, ssd_conv_w[layer], ssd_conv_b[layer],
                          ssd_dt_bias[layer], ssd_a_log[layer], ssd_d[layer], ssd_norm_w[layer])
        y_att = swa_sink_attention(q, k, v, attn_sinks[layer], rel_bias)
        y_pool = pool_mixer(u, pool_w[layer], pool_scale[layer])
        g_ssd, g_att, g_pool = jnp.split(jax.nn.sigmoid(gates), N_BRANCH, axis=-1)
        merged = (g_ssd * (y_ssd @ w_up_ssd[layer])
                  + g_att * (y_att @ w_up_attn[layer])
                  + g_pool * (y_pool @ w_up_pool[layer]))
        x = x + merged @ w_out[layer]
        h = rmsnorm(x, norm2_w[layer])
        x = x + jnp.square(jax.nn.relu(h @ w_mlp_in[layer])) @ w_mlp_out[layer]
    return rmsnorm(x, final_norm_w)
```

```python
import functools
import math

import numpy as np
import jax
import jax.numpy as jnp
from jax import lax
from jax.experimental import pallas as pl
from jax.experimental.pallas import tpu as pltpu

F32 = jnp.float32
BF16 = jnp.bfloat16

D_MODEL = 1024
EPS = 1e-6

SSD_HEAD_DIM = 64
SSD_N_HEADS = 16
SSD_N_GROUPS = 2
SSD_D_STATE = 128
SSD_CONV = 4
SSD_GROUP_WIDTH = D_MODEL // SSD_N_GROUPS
SSD_CONV_DIM = D_MODEL + 2 * SSD_N_GROUPS * SSD_D_STATE

ATT_HEAD_DIM = 64
ATT_N_HEADS = 8
ATT_N_KV = 2
ATT_CHUNK = 64
ATT_PAD = 128
ATT_BAND = ATT_PAD + ATT_CHUNK
N_BUCKETS = 32
MAX_DISTANCE = 128

POOL_WIDTH = 512
POOL_WINDOWS = (2, 4, 8, 16)
POOL_GROUP = 128
POOL_HIST = 16
CONV_HIST = 8

D_FF = 4 * D_MODEL

LANES = 128

_GATE_END = 3 * D_MODEL
_Z_END = _GATE_END + D_MODEL
_XBC_END = _Z_END + SSD_CONV_DIM
_DT_END = _XBC_END + SSD_N_HEADS
_Q_END = _DT_END + ATT_N_HEADS * ATT_HEAD_DIM
_K_END = _Q_END + ATT_N_KV * ATT_HEAD_DIM
_V_END = _K_END + ATT_N_KV * ATT_HEAD_DIM
_IN_COLS = _V_END + POOL_WIDTH

C_GATE = 0
C_Z = C_GATE + 3 * D_MODEL
C_XBC = C_Z + D_MODEL
C_Q = C_XBC + SSD_CONV_DIM
C_K = C_Q + ATT_N_HEADS * ATT_HEAD_DIM
C_V = C_K + LANES
C_U = C_V + LANES
C_DT = C_U + POOL_WIDTH
C_END = C_DT + LANES

SEQ_TILE = 256
MLP_TILE = 512
VMEM_LIMIT = 56 * 1024 * 1024


def _att_head_perm():
    cols = np.arange(ATT_N_HEADS * ATT_HEAD_DIM)
    tile, half, d = cols // LANES, (cols % LANES) // ATT_HEAD_DIM, cols % ATT_HEAD_DIM
    return (tile + (ATT_N_HEADS // ATT_N_KV) * half) * ATT_HEAD_DIM + d


def _t5_bucket_table():
    nb = N_BUCKETS // 2
    qpos = jnp.arange(ATT_CHUNK, dtype=jnp.int32)
    kpos = jnp.arange(ATT_BAND, dtype=jnp.int32) - ATT_PAD
    rel = kpos[None, :] - qpos[:, None]
    out = (rel > 0).astype(jnp.int32) * nb
    n = jnp.abs(rel)
    max_exact = nb // 2
    nf = jnp.maximum(n, 1).astype(jnp.float32)
    large = max_exact + (jnp.log(nf / max_exact) / math.log(MAX_DISTANCE / max_exact)
                         * (nb - max_exact)).astype(jnp.int32)
    large = jnp.minimum(large, nb - 1)
    return out + jnp.where(n < max_exact, n, large)


def _rmsnorm(x, w):
    return x * lax.rsqrt(jnp.mean(x * x, axis=-1, keepdims=True) + EPS) * w


def _dot(a, b):
    return jnp.dot(a, b, preferred_element_type=F32)


def _dot_nt(a, b):
    return lax.dot_general(a, b, (((1,), (1,)), ((), ())), preferred_element_type=F32)


def _dot_tn(a, b):
    return lax.dot_general(a, b, (((0,), (0,)), ((), ())), preferred_element_type=F32)


def _mixer_kernel(x_ref, n1_ref, win_ref, convw_ref, convb_ref, dtb_ref, alog_ref, dskip_ref,
                  ssdnw_ref, sink_ref, bias_ref, poolw_ref, pools_ref, wus_ref, wua_ref, wup_ref,
                  wout_ref, o_ref, xbc_buf, k_buf, v_buf, u_buf, state_ref, *, tile):
    T = tile
    j = pl.program_id(1)
    t0 = j * T

    @pl.when(j == 0)
    def _():
        xbc_buf[0:CONV_HIST, :] = jnp.zeros((CONV_HIST, SSD_CONV_DIM), F32)
        k_buf[0:ATT_PAD, :] = jnp.zeros((ATT_PAD, LANES), BF16)
        v_buf[0:ATT_PAD, :] = jnp.zeros((ATT_PAD, LANES), BF16)
        u_buf[0:POOL_HIST, :] = jnp.zeros((POOL_HIST, POOL_WIDTH), F32)
        state_ref[...] = jnp.zeros(state_ref.shape, F32)

    x = x_ref[...]
    hb = _rmsnorm(x, n1_ref[...]).astype(BF16)

    def proj(a, b):
        return _dot(hb, win_ref[:, a:b])

    lane_lo = lax.broadcasted_iota(jnp.int32, (1, LANES), 1) < (LANES // 2)

    xbc_buf[CONV_HIST:CONV_HIST + T, :] = proj(C_XBC, C_Q)
    conv = convb_ref[...]
    for jj in range(SSD_CONV):
        off = CONV_HIST - (SSD_CONV - 1) + jj
        conv = conv + convw_ref[jj:jj + 1, :] * xbc_buf[off:off + T, :]
    xbc = conv * jax.nn.sigmoid(conv)
    xs = xbc[:, :D_MODEL]
    bmat = xbc[:, D_MODEL:D_MODEL + SSD_N_GROUPS * SSD_D_STATE].astype(BF16)
    cmat = xbc[:, D_MODEL + SSD_N_GROUPS * SSD_D_STATE:].astype(BF16)

    dt = jax.nn.softplus(proj(C_DT, C_END) + dtb_ref[...])
    da = dt * (-jnp.exp(alog_ref[...]))
    row_i = lax.broadcasted_iota(jnp.int32, (T, T), 0)
    col_i = lax.broadcasted_iota(jnp.int32, (T, T), 1)
    causal = row_i >= col_i
    acs = jnp.dot(causal.astype(F32), da, preferred_element_type=F32,
                  precision=lax.Precision.HIGHEST)
    acs_t = acs.T

    def head_cols(arr):
        return [jnp.broadcast_to(arr[:, h:h + 1], (T, LANES)) for h in range(SSD_N_HEADS)]

    def per_channel(cols):
        return jnp.concatenate([jnp.where(lane_lo, cols[2 * p], cols[2 * p + 1])
                                for p in range(SSD_N_HEADS // 2)], axis=1)

    acs_cols = head_cols(acs)
    acs_ch = per_channel(acs_cols)
    dt_ch = per_channel(head_cols(dt))
    xdt = xs * dt_ch
    last = acs_ch[T - 1:T, :]
    grow = jnp.exp(acs_ch)
    tail = jnp.exp(last - acs_ch)
    tile_decay = jnp.exp(last)
    xdt_b = xdt.astype(BF16)
    xdt_tail_b = (xdt * tail).astype(BF16)

    y_tiles = []
    for g in range(SSD_N_GROUPS):
        gs = slice(g * SSD_GROUP_WIDTH, (g + 1) * SSD_GROUP_WIDTH)
        bm_g = bmat[:, g * SSD_D_STATE:(g + 1) * SSD_D_STATE]
        cm_g = cmat[:, g * SSD_D_STATE:(g + 1) * SSD_D_STATE]
        cb = _dot_nt(cm_g, bm_g)
        state = state_ref[g]
        y_off = _dot(cm_g, state.astype(BF16)) * grow[:, gs]
        state_ref[g] = state * tile_decay[:, gs] + _dot_tn(bm_g, xdt_tail_b[:, gs])
        for pp in range(SSD_GROUP_WIDTH // LANES):
            p = g * (SSD_GROUP_WIDTH // LANES) + pp
            blk = xdt_b[:, p * LANES:(p + 1) * LANES]
            y_pair = y_off[:, pp * LANES:(pp + 1) * LANES]
            for half in range(2):
                h = 2 * p + half
                col = acs_cols[h]
                seg = jnp.concatenate([col] * (T // LANES), axis=1) - acs_t[h:h + 1, :]
                lmat = jnp.exp(jnp.where(causal, seg, -jnp.inf))
                m_h = (cb * lmat).astype(BF16)
                keep = lane_lo if half == 0 else jnp.logical_not(lane_lo)
                y_pair = y_pair + _dot(m_h, jnp.where(keep, blk, jnp.zeros_like(blk)))
            y_tiles.append(y_pair)
    y = jnp.concatenate(y_tiles, axis=1) + xs * dskip_ref[...]
    z = proj(C_Z, C_XBC)
    y = y * (z * jax.nn.sigmoid(z))
    y_norm = []
    for g in range(SSD_N_GROUPS):
        yg = y[:, g * SSD_GROUP_WIDTH:(g + 1) * SSD_GROUP_WIDTH]
        y_norm.append(yg * lax.rsqrt(jnp.mean(yg * yg, axis=-1, keepdims=True) + EPS))
    y_ssd = (jnp.concatenate(y_norm, axis=1) * ssdnw_ref[...]).astype(BF16)

    q = (proj(C_Q, C_K) * (ATT_HEAD_DIM ** -0.5)).astype(BF16)
    k_buf[ATT_PAD:ATT_PAD + T, :] = proj(C_K, C_V).astype(BF16)
    v_buf[ATT_PAD:ATT_PAD + T, :] = proj(C_V, C_U).astype(BF16)
    key_i = lax.broadcasted_iota(jnp.int32, (1, ATT_BAND), 1)
    att_rows = []
    for c in range(T // ATT_CHUNK):
        r0 = c * ATT_CHUNK
        kband = k_buf[r0:r0 + ATT_BAND, :]
        vband = v_buf[r0:r0 + ATT_BAND, :]
        valid = (key_i + (r0 - ATT_PAD) + t0) >= 0
        tiles = []
        for jb in range(ATT_N_HEADS // ATT_N_KV):
            qblk = q[r0:r0 + ATT_CHUNK, jb * LANES:(jb + 1) * LANES]
            halves = []
            for half in range(ATT_N_KV):
                hq = jb + (ATT_N_HEADS // ATT_N_KV) * half
                keep = lane_lo if half == 0 else jnp.logical_not(lane_lo)
                s = _dot_nt(jnp.where(keep, qblk, jnp.zeros_like(qblk)), kband) + bias_ref[hq]
                s = jnp.where(valid, s, -jnp.inf)
                sink = sink_ref[hq]
                m = jnp.maximum(jnp.max(s, axis=-1, keepdims=True), sink)
                pr = jnp.exp(s - m)
                den = jnp.sum(pr, axis=-1, keepdims=True) + jnp.exp(sink - m)
                halves.append(_dot(pr.astype(BF16), vband) / den)
            tiles.append(jnp.where(lane_lo, halves[0], halves[1]))
        att_rows.append(jnp.concatenate(tiles, axis=1))
    y_att = jnp.concatenate(att_rows, axis=0).astype(BF16)

    u = proj(C_U, C_DT)
    u_buf[POOL_HIST:POOL_HIST + T, :] = u
    tpos = t0 + lax.broadcasted_iota(jnp.int32, (T, 1), 0)
    pool_tiles = []
    for gi, w in enumerate(POOL_WINDOWS):
        cs = slice(gi * POOL_GROUP, (gi + 1) * POOL_GROUP)
        ug = u[:, cs]
        acc = ug
        for back in range(1, w):
            acc = acc + u_buf[POOL_HIST - back:POOL_HIST - back + T, cs]
        cnt = jnp.minimum(tpos + 1, w).astype(F32)
        pooled = acc / cnt - ug
        pool_tiles.append(_dot(pooled.astype(BF16), poolw_ref[gi]))
    y_pool = (jnp.concatenate(pool_tiles, axis=1) * pools_ref[...]).astype(BF16)

    merged = jax.nn.sigmoid(proj(C_GATE, C_GATE + D_MODEL)) * _dot(y_ssd, wus_ref[...])
    merged = merged + jax.nn.sigmoid(proj(C_GATE + D_MODEL, C_GATE + 2 * D_MODEL)) * _dot(y_att, wua_ref[...])
    merged = merged + jax.nn.sigmoid(proj(C_GATE + 2 * D_MODEL, C_Z)) * _dot(y_pool, wup_ref[...])
    o_ref[...] = x + _dot(merged.astype(BF16), wout_ref[...])

    xbc_buf[0:CONV_HIST, :] = xbc_buf[T:T + CONV_HIST, :]
    k_buf[0:ATT_PAD, :] = k_buf[T:T + ATT_PAD, :]
    v_buf[0:ATT_PAD, :] = v_buf[T:T + ATT_PAD, :]
    u_buf[0:POOL_HIST, :] = u_buf[T:T + POOL_HIST, :]


def _mlp_kernel(x_ref, n2_ref, w1_ref, w2_ref, nf_ref, o_ref, *, final_norm):
    x = x_ref[...]
    hb = _rmsnorm(x, n2_ref[...]).astype(BF16)
    a = jnp.maximum(_dot(hb, w1_ref[...]), 0.0)
    out = x + _dot((a * a).astype(BF16), w2_ref[...])
    if final_norm:
        out = _rmsnorm(out, nf_ref[...])
    o_ref[...] = out


def _resident(shape, index):
    return pl.BlockSpec(shape, index, pipeline_mode=pl.Buffered(1))


def _mixer_call(x, layer, p, tile):
    b, s, _ = x.shape
    assert s % tile == 0 and tile % LANES == 0 and tile >= ATT_PAD
    l3 = lambda i, j: (layer, 0, 0)
    l4 = lambda i, j: (layer, 0, 0, 0)
    vec = lambda n: _resident((None, 1, n), l3)
    in_specs = [
        pl.BlockSpec((None, tile, D_MODEL), lambda i, j: (i, j, 0)),
        vec(D_MODEL),
        _resident((None, D_MODEL, C_END), l3),
        _resident((None, SSD_CONV, SSD_CONV_DIM), l3),
        vec(SSD_CONV_DIM),
        vec(LANES),
        vec(LANES),
        vec(D_MODEL),
        vec(D_MODEL),
        pl.BlockSpec(memory_space=pltpu.SMEM),
        _resident((ATT_N_HEADS, ATT_CHUNK, ATT_BAND), lambda i, j: (0, 0, 0)),
        _resident((None, len(POOL_WINDOWS), POOL_GROUP, POOL_GROUP), l4),
        vec(POOL_WIDTH),
        _resident((None, D_MODEL, D_MODEL), l3),
        _resident((None, ATT_N_HEADS * ATT_HEAD_DIM, D_MODEL), l3),
        _resident((None, POOL_WIDTH, D_MODEL), l3),
        _resident((None, D_MODEL, D_MODEL), l3),
    ]
    return pl.pallas_call(
        functools.partial(_mixer_kernel, tile=tile),
        grid=(b, s // tile),
        in_specs=in_specs,
        out_specs=pl.BlockSpec((None, tile, D_MODEL), lambda i, j: (i, j, 0)),
        out_shape=jax.ShapeDtypeStruct(x.shape, F32),
        scratch_shapes=[
            pltpu.VMEM((CONV_HIST + tile, SSD_CONV_DIM), F32),
            pltpu.VMEM((ATT_PAD + tile, LANES), BF16),
            pltpu.VMEM((ATT_PAD + tile, LANES), BF16),
            pltpu.VMEM((POOL_HIST + tile, POOL_WIDTH), F32),
            pltpu.VMEM((SSD_N_GROUPS, SSD_D_STATE, SSD_GROUP_WIDTH), F32),
        ],
        compiler_params=pltpu.CompilerParams(
            dimension_semantics=("arbitrary", "arbitrary"), vmem_limit_bytes=VMEM_LIMIT),
        name="mixer",
    )(x, p["norm1_w"], p["w_in"], p["conv_w"], p["conv_b"], p["dt_bias"], p["a_log"], p["d_skip"],
      p["ssd_norm_w"], p["sinks"][layer], p["att_bias"], p["pool_w"], p["pool_scale"],
      p["w_up_ssd"], p["w_up_attn"], p["w_up_pool"], p["w_out"])


def _mlp_call(x2d, layer, p, tile, final_norm):
    n, _ = x2d.shape
    assert n % tile == 0
    l3 = lambda i: (layer, 0, 0)
    return pl.pallas_call(
        functools.partial(_mlp_kernel, final_norm=final_norm),
        grid=(n // tile,),
        in_specs=[
            pl.BlockSpec((tile, D_MODEL), lambda i: (i, 0)),
            _resident((None, 1, D_MODEL), l3),
            _resident((None, D_MODEL, D_FF), l3),
            _resident((None, D_FF, D_MODEL), l3),
            _resident((1, D_MODEL), lambda i: (0, 0)),
        ],
        out_specs=pl.BlockSpec((tile, D_MODEL), lambda i: (i, 0)),
        out_shape=jax.ShapeDtypeStruct(x2d.shape, F32),
        compiler_params=pltpu.CompilerParams(
            dimension_semantics=("arbitrary",), vmem_limit_bytes=VMEM_LIMIT),
        name="mlp",
    )(x2d, p["norm2_w"], p["w_mlp_in"], p["w_mlp_out"], p["final_norm_w"])


def _prepare_params(norm1_w, w_in, ssd_conv_w, ssd_conv_b, ssd_dt_bias, ssd_a_log, ssd_d, ssd_norm_w,
                    attn_sinks, rel_bias, pool_w, pool_scale, w_up_ssd, w_up_attn, w_up_pool, w_out,
                    norm2_w, w_mlp_in, w_mlp_out, final_norm_w):
    depth = w_in.shape[0]
    perm = _att_head_perm()
    seg = lambda a, b: w_in[:, :, a:b]
    pad_lanes = lambda a: jnp.pad(a, ((0, 0),) * (a.ndim - 1) + ((0, LANES - a.shape[-1]),))
    w_in_k = jnp.concatenate([
        seg(0, _GATE_END), seg(_GATE_END, _Z_END), seg(_Z_END, _XBC_END),
        seg(_DT_END, _Q_END)[:, :, perm], seg(_Q_END, _K_END), seg(_K_END, _V_END),
        seg(_V_END, _IN_COLS), pad_lanes(seg(_XBC_END, _DT_END))], axis=-1).astype(BF16)
    row = lambda a: a.astype(F32)[:, None, :]
    att_bias = jnp.moveaxis(rel_bias.astype(F32)[_t5_bucket_table()], -1, 0)
    return {
        "norm1_w": row(norm1_w), "w_in": w_in_k,
        "conv_w": ssd_conv_w.astype(F32), "conv_b": row(ssd_conv_b),
        "dt_bias": row(pad_lanes(ssd_dt_bias)), "a_log": row(pad_lanes(ssd_a_log)),
        "d_skip": row(jnp.repeat(ssd_d, SSD_HEAD_DIM, axis=-1)), "ssd_norm_w": row(ssd_norm_w),
        "sinks": attn_sinks.astype(F32), "att_bias": att_bias,
        "pool_w": pool_w.astype(BF16), "pool_scale": row(pool_scale),
        "w_up_ssd": w_up_ssd.astype(BF16), "w_up_attn": w_up_attn[:, perm, :].astype(BF16),
        "w_up_pool": w_up_pool.astype(BF16), "w_out": w_out.astype(BF16),
        "norm2_w": row(norm2_w), "w_mlp_in": w_mlp_in.astype(BF16), "w_mlp_out": w_mlp_out.astype(BF16),
        "final_norm_w": final_norm_w.astype(F32)[None, :], "depth": depth,
    }


def kernel(x, norm1_w, w_in, ssd_conv_w, ssd_conv_b, ssd_dt_bias, ssd_a_log, ssd_d, ssd_norm_w, attn_sinks, rel_bias, pool_w, pool_scale, w_up_ssd, w_up_attn, w_up_pool, w_out, norm2_w, w_mlp_in, w_mlp_out, final_norm_w):
    p = _prepare_params(norm1_w, w_in, ssd_conv_w, ssd_conv_b, ssd_dt_bias, ssd_a_log, ssd_d, ssd_norm_w,
                        attn_sinks, rel_bias, pool_w, pool_scale, w_up_ssd, w_up_attn, w_up_pool, w_out,
                        norm2_w, w_mlp_in, w_mlp_out, final_norm_w)
    b, s, d = x.shape
    seq_tile = min(SEQ_TILE, s)
    mlp_tile = min(MLP_TILE, b * s)
    depth = p.pop("depth")
    x = x.astype(F32)
    for layer in range(depth):
        x = _mixer_call(x, layer, p, seq_tile)
        x = _mlp_call(x.reshape(b * s, d), layer, p, mlp_tile, layer == depth - 1).reshape(b, s, d)
    return x
```

```python
import functools
import math

import numpy as np
import jax
import jax.numpy as jnp
from jax import lax
from jax.experimental import pallas as pl
from jax.experimental.pallas import tpu as pltpu

F32 = jnp.float32
BF16 = jnp.bfloat16

D_MODEL = 1024
EPS = 1e-6

SSD_HEAD_DIM = 64
SSD_N_HEADS = 16
SSD_N_GROUPS = 2
SSD_D_STATE = 128
SSD_CONV = 4
SSD_GROUP_WIDTH = D_MODEL // SSD_N_GROUPS
SSD_CONV_DIM = D_MODEL + 2 * SSD_N_GROUPS * SSD_D_STATE

ATT_HEAD_DIM = 64
ATT_N_HEADS = 8
ATT_N_KV = 2
ATT_CHUNK = 64
ATT_PAD = 128
ATT_BAND = ATT_PAD + ATT_CHUNK
N_BUCKETS = 32
MAX_DISTANCE = 128

POOL_WIDTH = 512
POOL_WINDOWS = (2, 4, 8, 16)
POOL_GROUP = 128
POOL_HIST = 16
CONV_HIST = 8

D_FF = 4 * D_MODEL

LANES = 128

_GATE_END = 3 * D_MODEL
_Z_END = _GATE_END + D_MODEL
_XBC_END = _Z_END + SSD_CONV_DIM
_DT_END = _XBC_END + SSD_N_HEADS
_Q_END = _DT_END + ATT_N_HEADS * ATT_HEAD_DIM
_K_END = _Q_END + ATT_N_KV * ATT_HEAD_DIM
_V_END = _K_END + ATT_N_KV * ATT_HEAD_DIM
_IN_COLS = _V_END + POOL_WIDTH

C_GATE = 0
C_Z = C_GATE + 3 * D_MODEL
C_XBC = C_Z + D_MODEL
C_Q = C_XBC + SSD_CONV_DIM
C_K = C_Q + ATT_N_HEADS * ATT_HEAD_DIM
C_V = C_K + LANES
C_U = C_V + LANES
C_DT = C_U + POOL_WIDTH
C_END = C_DT + LANES

SEQ_TILE = 256
MLP_TILE = 512
VMEM_LIMIT = 56 * 1024 * 1024


def _att_head_perm():
    cols = np.arange(ATT_N_HEADS * ATT_HEAD_DIM)
    tile, half, d = cols // LANES, (cols % LANES) // ATT_HEAD_DIM, cols % ATT_HEAD_DIM
    return (tile + (ATT_N_HEADS // ATT_N_KV) * half) * ATT_HEAD_DIM + d


def _t5_bucket_table():
    nb = N_BUCKETS // 2
    qpos = jnp.arange(ATT_CHUNK, dtype=jnp.int32)
    kpos = jnp.arange(ATT_BAND, dtype=jnp.int32) - ATT_PAD
    rel = kpos[None, :] - qpos[:, None]
    out = (rel > 0).astype(jnp.int32) * nb
    n = jnp.abs(rel)
    max_exact = nb // 2
    nf = jnp.maximum(n, 1).astype(jnp.float32)
    large = max_exact + (jnp.log(nf / max_exact) / math.log(MAX_DISTANCE / max_exact)
                         * (nb - max_exact)).astype(jnp.int32)
    large = jnp.minimum(large, nb - 1)
    return out + jnp.where(n < max_exact, n, large)


def _rmsnorm(x, w):
    return x * lax.rsqrt(jnp.mean(x * x, axis=-1, keepdims=True) + EPS) * w


def _dot(a, b):
    return jnp.dot(a, b, preferred_element_type=F32)


def _dot_nt(a, b):
    return lax.dot_general(a, b, (((1,), (1,)), ((), ())), preferred_element_type=F32)


def _dot_tn(a, b):
    return lax.dot_general(a, b, (((0,), (0,)), ((), ())), preferred_element_type=F32)


def _mixer_kernel(x_ref, n1_ref, win_ref, convw_ref, convb_ref, dtb_ref, alog_ref, dskip_ref,
                  ssdnw_ref, sink_ref, bias_ref, poolw_ref, pools_ref, wus_ref, wua_ref, wup_ref,
                  wout_ref, o_ref, xbc_buf, k_buf, v_buf, u_buf, state_ref, *, tile):
    T = tile
    j = pl.program_id(1)
    t0 = j * T

    @pl.when(j == 0)
    def _():
        xbc_buf[0:CONV_HIST, :] = jnp.zeros((CONV_HIST, SSD_CONV_DIM), F32)
        k_buf[0:ATT_PAD, :] = jnp.zeros((ATT_PAD, LANES), BF16)
        v_buf[0:ATT_PAD, :] = jnp.zeros((ATT_PAD, LANES), BF16)
        u_buf[0:POOL_HIST, :] = jnp.zeros((POOL_HIST, POOL_WIDTH), F32)
        state_ref[...] = jnp.zeros(state_ref.shape, F32)

    x = x_ref[...]
    hb = _rmsnorm(x, n1_ref[...]).astype(BF16)

    def proj(a, b):
        return _dot(hb, win_ref[:, a:b])

    lane_lo = lax.broadcasted_iota(jnp.int32, (1, LANES), 1) < (LANES // 2)

    xbc_buf[CONV_HIST:CONV_HIST + T, :] = proj(C_XBC, C_Q)
    conv = convb_ref[...]
    for jj in range(SSD_CONV):
        off = CONV_HIST - (SSD_CONV - 1) + jj
        conv = conv + convw_ref[jj:jj + 1, :] * xbc_buf[off:off + T, :]
    xbc = conv * jax.nn.sigmoid(conv)
    xs = xbc[:, :D_MODEL]
    bmat = xbc[:, D_MODEL:D_MODEL + SSD_N_GROUPS * SSD_D_STATE].astype(BF16)
    cmat = xbc[:, D_MODEL + SSD_N_GROUPS * SSD_D_STATE:].astype(BF16)

    dt = jax.nn.softplus(proj(C_DT, C_END) + dtb_ref[...])
    da = dt * (-jnp.exp(alog_ref[...]))
    row_i = lax.broadcasted_iota(jnp.int32, (T, T), 0)
    col_i = lax.broadcasted_iota(jnp.int32, (T, T), 1)
    causal = row_i >= col_i
    acs = jnp.dot(causal.astype(F32), da, preferred_element_type=F32,
                  precision=lax.Precision.HIGHEST)
    acs_t = acs.T

    def head_cols(arr):
        return [jnp.broadcast_to(arr[:, h:h + 1], (T, LANES)) for h in range(SSD_N_HEADS)]

    def per_channel(cols):
        return jnp.concatenate([jnp.where(lane_lo, cols[2 * p], cols[2 * p + 1])
                                for p in range(SSD_N_HEADS // 2)], axis=1)

    acs_cols = head_cols(acs)
    acs_ch = per_channel(acs_cols)
    dt_ch = per_channel(head_cols(dt))
    xdt = xs * dt_ch
    last = acs_ch[T - 1:T, :]
    grow = jnp.exp(acs_ch)
    tail = jnp.exp(last - acs_ch)
    tile_decay = jnp.exp(last)
    xdt_b = xdt.astype(BF16)
    xdt_tail_b = (xdt * tail).astype(BF16)

    y_tiles = []
    for g in range(SSD_N_GROUPS):
        gs = slice(g * SSD_GROUP_WIDTH, (g + 1) * SSD_GROUP_WIDTH)
        bm_g = bmat[:, g * SSD_D_STATE:(g + 1) * SSD_D_STATE]
        cm_g = cmat[:, g * SSD_D_STATE:(g + 1) * SSD_D_STATE]
        cb = _dot_nt(cm_g, bm_g)
        state = state_ref[g]
        y_off = _dot(cm_g, state.astype(BF16)) * grow[:, gs]
        state_ref[g] = state * tile_decay[:, gs] + _dot_tn(bm_g, xdt_tail_b[:, gs])
        for pp in range(SSD_GROUP_WIDTH // LANES):
            p = g * (SSD_GROUP_WIDTH // LANES) + pp
            blk = xdt_b[:, p * LANES:(p + 1) * LANES]
            y_pair = y_off[:, pp * LANES:(pp + 1) * LANES]
            for half in range(2):
                h = 2 * p + half
                col = acs_cols[h]
                seg = jnp.concatenate([col] * (T // LANES), axis=1) - acs_t[h:h + 1, :]
                lmat = jnp.exp(jnp.where(causal, seg, -jnp.inf))
                m_h = (cb * lmat).astype(BF16)
                keep = lane_lo if half == 0 else jnp.logical_not(lane_lo)
                y_pair = y_pair + _dot(m_h, jnp.where(keep, blk, jnp.zeros_like(blk)))
            y_tiles.append(y_pair)
    y = jnp.concatenate(y_tiles, axis=1) + xs * dskip_ref[...]
    z = proj(C_Z, C_XBC)
    y = y * (z * jax.nn.sigmoid(z))
    y_norm = []
    for g in range(SSD_N_GROUPS):
        yg = y[:, g * SSD_GROUP_WIDTH:(g + 1) * SSD_GROUP_WIDTH]
        y_norm.append(yg * lax.rsqrt(jnp.mean(yg * yg, axis=-1, keepdims=True) + EPS))
    y_ssd = (jnp.concatenate(y_norm, axis=1) * ssdnw_ref[...]).astype(BF16)

    q = (proj(C_Q, C_K) * (ATT_HEAD_DIM ** -0.5)).astype(BF16)
    k_buf[ATT_PAD:ATT_PAD + T, :] = proj(C_K, C_V).astype(BF16)
    v_buf[ATT_PAD:ATT_PAD + T, :] = proj(C_V, C_U).astype(BF16)
    key_i = lax.broadcasted_iota(jnp.int32, (1, ATT_BAND), 1)
    rep = ATT_N_HEADS // ATT_N_KV
    sink_rows = jnp.concatenate([jnp.full((ATT_CHUNK, LANES), sink_ref[hq], F32)
                                 for hq in range(ATT_N_HEADS)], axis=0)
    att_rows = []
    for c in range(T // ATT_CHUNK):
        r0 = c * ATT_CHUNK
        kband = k_buf[r0:r0 + ATT_BAND, :]
        vband = v_buf[r0:r0 + ATT_BAND, :]
        valid = (key_i + (r0 - ATT_PAD) + t0) >= 0
        q_stack = []
        for hq in range(ATT_N_HEADS):
            qblk = q[r0:r0 + ATT_CHUNK, (hq % rep) * LANES:(hq % rep + 1) * LANES]
            keep = lane_lo if hq < rep else jnp.logical_not(lane_lo)
            q_stack.append(jnp.where(keep, qblk, jnp.zeros_like(qblk)))
        s = _dot_nt(jnp.concatenate(q_stack, axis=0), kband) + bias_ref[...]
        s = jnp.where(valid, s, -jnp.inf)
        m = jnp.maximum(jnp.max(s, axis=-1, keepdims=True), sink_rows)
        pr = jnp.exp(s - jnp.concatenate([m, m[:, :ATT_BAND - LANES]], axis=1))
        den = jnp.sum(pr, axis=-1, keepdims=True) + jnp.exp(sink_rows - m)
        o = _dot(pr.astype(BF16), vband) * (1.0 / den)
        att_rows.append(jnp.concatenate(
            [jnp.where(lane_lo, o[jb * ATT_CHUNK:(jb + 1) * ATT_CHUNK],
                       o[(jb + rep) * ATT_CHUNK:(jb + rep + 1) * ATT_CHUNK]) for jb in range(rep)], axis=1))
    y_att = jnp.concatenate(att_rows, axis=0).astype(BF16)

    u = proj(C_U, C_DT)
    u_buf[POOL_HIST:POOL_HIST + T, :] = u
    tpos = t0 + lax.broadcasted_iota(jnp.int32, (T, 1), 0)
    pool_tiles = []
    for gi, w in enumerate(POOL_WINDOWS):
        cs = slice(gi * POOL_GROUP, (gi + 1) * POOL_GROUP)
        ug = u[:, cs]
        acc = ug
        for back in range(1, w):
            acc = acc + u_buf[POOL_HIST - back:POOL_HIST - back + T, cs]
        cnt = jnp.minimum(tpos + 1, w).astype(F32)
        pooled = acc / cnt - ug
        pool_tiles.append(_dot(pooled.astype(BF16), poolw_ref[gi]))
    y_pool = (jnp.concatenate(pool_tiles, axis=1) * pools_ref[...]).astype(BF16)

    merged = jax.nn.sigmoid(proj(C_GATE, C_GATE + D_MODEL)) * _dot(y_ssd, wus_ref[...])
    merged = merged + jax.nn.sigmoid(proj(C_GATE + D_MODEL, C_GATE + 2 * D_MODEL)) * _dot(y_att, wua_ref[...])
    merged = merged + jax.nn.sigmoid(proj(C_GATE + 2 * D_MODEL, C_Z)) * _dot(y_pool, wup_ref[...])
    o_ref[...] = x + _dot(merged.astype(BF16), wout_ref[...])

    xbc_buf[0:CONV_HIST, :] = xbc_buf[T:T + CONV_HIST, :]
    k_buf[0:ATT_PAD, :] = k_buf[T:T + ATT_PAD, :]
    v_buf[0:ATT_PAD, :] = v_buf[T:T + ATT_PAD, :]
    u_buf[0:POOL_HIST, :] = u_buf[T:T + POOL_HIST, :]


def _mlp_kernel(x_ref, n2_ref, w1_ref, w2_ref, nf_ref, o_ref, *, final_norm):
    x = x_ref[...]
    hb = _rmsnorm(x, n2_ref[...]).astype(BF16)
    a = jnp.maximum(_dot(hb, w1_ref[...]), 0.0)
    out = x + _dot((a * a).astype(BF16), w2_ref[...])
    if final_norm:
        out = _rmsnorm(out, nf_ref[...])
    o_ref[...] = out


def _resident(shape, index):
    return pl.BlockSpec(shape, index, pipeline_mode=pl.Buffered(1))


def _mixer_call(x, layer, p, tile):
    b, s, _ = x.shape
    assert s % tile == 0 and tile % LANES == 0 and tile >= ATT_PAD
    l3 = lambda i, j: (layer, 0, 0)
    l4 = lambda i, j: (layer, 0, 0, 0)
    vec = lambda n: _resident((None, 1, n), l3)
    in_specs = [
        pl.BlockSpec((None, tile, D_MODEL), lambda i, j: (i, j, 0)),
        vec(D_MODEL),
        _resident((None, D_MODEL, C_END), l3),
        _resident((None, SSD_CONV, SSD_CONV_DIM), l3),
        vec(SSD_CONV_DIM),
        vec(LANES),
        vec(LANES),
        vec(D_MODEL),
        vec(D_MODEL),
        pl.BlockSpec(memory_space=pltpu.SMEM),
        _resident((ATT_N_HEADS * ATT_CHUNK, ATT_BAND), lambda i, j: (0, 0)),
        _resident((None, len(POOL_WINDOWS), POOL_GROUP, POOL_GROUP), l4),
        vec(POOL_WIDTH),
        _resident((None, D_MODEL, D_MODEL), l3),
        _resident((None, ATT_N_HEADS * ATT_HEAD_DIM, D_MODEL), l3),
        _resident((None, POOL_WIDTH, D_MODEL), l3),
        _resident((None, D_MODEL, D_MODEL), l3),
    ]
    return pl.pallas_call(
        functools.partial(_mixer_kernel, tile=tile),
        grid=(b, s // tile),
        in_specs=in_specs,
        out_specs=pl.BlockSpec((None, tile, D_MODEL), lambda i, j: (i, j, 0)),
        out_shape=jax.ShapeDtypeStruct(x.shape, F32),
        scratch_shapes=[
            pltpu.VMEM((CONV_HIST + tile, SSD_CONV_DIM), F32),
            pltpu.VMEM((ATT_PAD + tile, LANES), BF16),
            pltpu.VMEM((ATT_PAD + tile, LANES), BF16),
            pltpu.VMEM((POOL_HIST + tile, POOL_WIDTH), F32),
            pltpu.VMEM((SSD_N_GROUPS, SSD_D_STATE, SSD_GROUP_WIDTH), F32),
        ],
        compiler_params=pltpu.CompilerParams(
            dimension_semantics=("arbitrary", "arbitrary"), vmem_limit_bytes=VMEM_LIMIT),
        name="mixer",
    )(x, p["norm1_w"], p["w_in"], p["conv_w"], p["conv_b"], p["dt_bias"], p["a_log"], p["d_skip"],
      p["ssd_norm_w"], p["sinks"][layer], p["att_bias"], p["pool_w"], p["pool_scale"],
      p["w_up_ssd"], p["w_up_attn"], p["w_up_pool"], p["w_out"])


def _mlp_call(x2d, layer, p, tile, final_norm):
    n, _ = x2d.shape
    assert n % tile == 0
    l3 = lambda i: (layer, 0, 0)
    return pl.pallas_call(
        functools.partial(_mlp_kernel, final_norm=final_norm),
        grid=(n // tile,),
        in_specs=[
            pl.BlockSpec((tile, D_MODEL), lambda i: (i, 0)),
            _resident((None, 1, D_MODEL), l3),
            _resident((None, D_MODEL, D_FF), l3),
            _resident((None, D_FF, D_MODEL), l3),
            _resident((1, D_MODEL), lambda i: (0, 0)),
        ],
        out_specs=pl.BlockSpec((tile, D_MODEL), lambda i: (i, 0)),
        out_shape=jax.ShapeDtypeStruct(x2d.shape, F32),
        compiler_params=pltpu.CompilerParams(
            dimension_semantics=("arbitrary",), vmem_limit_bytes=VMEM_LIMIT),
        name="mlp",
    )(x2d, p["norm2_w"], p["w_mlp_in"], p["w_mlp_out"], p["final_norm_w"])


def _prepare_params(norm1_w, w_in, ssd_conv_w, ssd_conv_b, ssd_dt_bias, ssd_a_log, ssd_d, ssd_norm_w,
                    attn_sinks, rel_bias, pool_w, pool_scale, w_up_ssd, w_up_attn, w_up_pool, w_out,
                    norm2_w, w_mlp_in, w_mlp_out, final_norm_w):
    depth = w_in.shape[0]
    perm = _att_head_perm()
    seg = lambda a, b: w_in[:, :, a:b]
    pad_lanes = lambda a: jnp.pad(a, ((0, 0),) * (a.ndim - 1) + ((0, LANES - a.shape[-1]),))
    w_in_k = jnp.concatenate([
        seg(0, _GATE_END), seg(_GATE_END, _Z_END), seg(_Z_END, _XBC_END),
        seg(_DT_END, _Q_END)[:, :, perm], seg(_Q_END, _K_END), seg(_K_END, _V_END),
        seg(_V_END, _IN_COLS), pad_lanes(seg(_XBC_END, _DT_END))], axis=-1).astype(BF16)
    row = lambda a: a.astype(F32)[:, None, :]
    att_bias = jnp.moveaxis(rel_bias.astype(F32)[_t5_bucket_table()], -1, 0)
    att_bias = att_bias.reshape(ATT_N_HEADS * ATT_CHUNK, ATT_BAND)
    return {
        "norm1_w": row(norm1_w), "w_in": w_in_k,
        "conv_w": ssd_conv_w.astype(F32), "conv_b": row(ssd_conv_b),
        "dt_bias": row(pad_lanes(ssd_dt_bias)), "a_log": row(pad_lanes(ssd_a_log)),
        "d_skip": row(jnp.repeat(ssd_d, SSD_HEAD_DIM, axis=-1)), "ssd_norm_w": row(ssd_norm_w),
        "sinks": attn_sinks.astype(F32), "att_bias": att_bias,
        "pool_w": pool_w.astype(BF16), "pool_scale": row(pool_scale),
        "w_up_ssd": w_up_ssd.astype(BF16), "w_up_attn": w_up_attn[:, perm, :].astype(BF16),
        "w_up_pool": w_up_pool.astype(BF16), "w_out": w_out.astype(BF16),
        "norm2_w": row(norm2_w), "w_mlp_in": w_mlp_in.astype(BF16), "w_mlp_out": w_mlp_out.astype(BF16),
        "final_norm_w": final_norm_w.astype(F32)[None, :], "depth": depth,
    }


def kernel(x, norm1_w, w_in, ssd_conv_w, ssd_conv_b, ssd_dt_bias, ssd_a_log, ssd_d, ssd_norm_w, attn_sinks, rel_bias, pool_w, pool_scale, w_up_ssd, w_up_attn, w_up_pool, w_out, norm2_w, w_mlp_in, w_mlp_out, final_norm_w):
    p = _prepare_params(norm1_w, w_in, ssd_conv_w, ssd_conv_b, ssd_dt_bias, ssd_a_log, ssd_d, ssd_norm_w,
                        attn_sinks, rel_bias, pool_w, pool_scale, w_up_ssd, w_up_attn, w_up_pool, w_out,
                        norm2_w, w_mlp_in, w_mlp_out, final_norm_w)
    b, s, d = x.shape
    seq_tile = min(SEQ_TILE, s)
    mlp_tile = min(MLP_TILE, b * s)
    depth = p.pop("depth")
    x = x.astype(F32)
    for layer in range(depth):
        x = _mixer_call(x, layer, p, seq_tile)
        x = _mlp_call(x.reshape(b * s, d), layer, p, mlp_tile, layer == depth - 1).reshape(b, s, d)
    return x
```

```python
import functools
import math

import numpy as np
import jax
import jax.numpy as jnp
from jax import lax
from jax.experimental import pallas as pl
from jax.experimental.pallas import tpu as pltpu

F32 = jnp.float32
BF16 = jnp.bfloat16

D_MODEL = 1024
EPS = 1e-6

SSD_HEAD_DIM = 64
SSD_N_HEADS = 16
SSD_N_GROUPS = 2
SSD_D_STATE = 128
SSD_CONV = 4
SSD_GROUP_WIDTH = D_MODEL // SSD_N_GROUPS
SSD_CONV_DIM = D_MODEL + 2 * SSD_N_GROUPS * SSD_D_STATE

ATT_HEAD_DIM = 64
ATT_N_HEADS = 8
ATT_N_KV = 2
ATT_CHUNK = 64
ATT_PAD = 128
ATT_BAND = ATT_PAD + ATT_CHUNK
N_BUCKETS = 32
MAX_DISTANCE = 128

POOL_WIDTH = 512
POOL_WINDOWS = (2, 4, 8, 16)
POOL_GROUP = 128
POOL_HIST = 16
CONV_HIST = 8

D_FF = 4 * D_MODEL

LANES = 128

_GATE_END = 3 * D_MODEL
_Z_END = _GATE_END + D_MODEL
_XBC_END = _Z_END + SSD_CONV_DIM
_DT_END = _XBC_END + SSD_N_HEADS
_Q_END = _DT_END + ATT_N_HEADS * ATT_HEAD_DIM
_K_END = _Q_END + ATT_N_KV * ATT_HEAD_DIM
_V_END = _K_END + ATT_N_KV * ATT_HEAD_DIM
_IN_COLS = _V_END + POOL_WIDTH

C_GATE = 0
C_Z = C_GATE + 3 * D_MODEL
C_XBC = C_Z + D_MODEL
C_Q = C_XBC + SSD_CONV_DIM
C_K = C_Q + ATT_N_HEADS * ATT_HEAD_DIM
C_V = C_K + LANES
C_U = C_V + LANES
C_DT = C_U + POOL_WIDTH
C_END = C_DT + LANES

SEQ_TILE = 256
MLP_TILE = 512
VMEM_LIMIT = 56 * 1024 * 1024


def _att_head_perm():
    cols = np.arange(ATT_N_HEADS * ATT_HEAD_DIM)
    tile, half, d = cols // LANES, (cols % LANES) // ATT_HEAD_DIM, cols % ATT_HEAD_DIM
    return (tile + (ATT_N_HEADS // ATT_N_KV) * half) * ATT_HEAD_DIM + d


def _t5_bucket_table():
    nb = N_BUCKETS // 2
    qpos = jnp.arange(ATT_CHUNK, dtype=jnp.int32)
    kpos = jnp.arange(ATT_BAND, dtype=jnp.int32) - ATT_PAD
    rel = kpos[None, :] - qpos[:, None]
    out = (rel > 0).astype(jnp.int32) * nb
    n = jnp.abs(rel)
    max_exact = nb // 2
    nf = jnp.maximum(n, 1).astype(jnp.float32)
    large = max_exact + (jnp.log(nf / max_exact) / math.log(MAX_DISTANCE / max_exact)
                         * (nb - max_exact)).astype(jnp.int32)
    large = jnp.minimum(large, nb - 1)
    return out + jnp.where(n < max_exact, n, large)


def _rmsnorm(x, w):
    return x * lax.rsqrt(jnp.mean(x * x, axis=-1, keepdims=True) + EPS) * w


def _dot(a, b):
    return jnp.dot(a, b, preferred_element_type=F32)


def _dot_nt(a, b):
    return lax.dot_general(a, b, (((1,), (1,)), ((), ())), preferred_element_type=F32)


def _dot_tn(a, b):
    return lax.dot_general(a, b, (((0,), (0,)), ((), ())), preferred_element_type=F32)


def _mixer_kernel(x_ref, n1_ref, win_ref, convw_ref, convb_ref, dtb_ref, alog_ref, dskip_ref,
                  ssdnw_ref, sink_ref, bias_ref, poolw_ref, pools_ref, wus_ref, wua_ref, wup_ref,
                  wout_ref, o_ref, xbc_buf, k_buf, v_buf, u_buf, state_ref, *, tile):
    T = tile
    j = pl.program_id(1)
    t0 = j * T

    @pl.when(j == 0)
    def _():
        xbc_buf[:, 0:CONV_HIST, :] = jnp.zeros((SSD_CONV_DIM // LANES, CONV_HIST, LANES), F32)
        k_buf[0:ATT_PAD, :] = jnp.zeros((ATT_PAD, LANES), BF16)
        v_buf[0:ATT_PAD, :] = jnp.zeros((ATT_PAD, LANES), BF16)
        u_buf[:, 0:POOL_HIST, :] = jnp.zeros((POOL_WIDTH // LANES, POOL_HIST, LANES), F32)
        state_ref[...] = jnp.zeros(state_ref.shape, F32)

    x = x_ref[...]
    hb = _rmsnorm(x, n1_ref[...]).astype(BF16)

    def proj(a, b):
        return _dot(hb, win_ref[:, a:b])

    lane_lo = lax.broadcasted_iota(jnp.int32, (1, LANES), 1) < (LANES // 2)

    gate_ssd = jax.nn.sigmoid(proj(C_GATE, C_GATE + D_MODEL))
    gate_att = jax.nn.sigmoid(proj(C_GATE + D_MODEL, C_GATE + 2 * D_MODEL))
    gate_pool = jax.nn.sigmoid(proj(C_GATE + 2 * D_MODEL, C_Z))

    xbc_raw = proj(C_XBC, C_Q)
    conv_blocks = []
    for cb in range(SSD_CONV_DIM // LANES):
        cs = slice(cb * LANES, (cb + 1) * LANES)
        xbc_buf[cb, CONV_HIST:CONV_HIST + T, :] = xbc_raw[:, cs]
        conv = convb_ref[:, cs]
        for jj in range(SSD_CONV):
            off = CONV_HIST - (SSD_CONV - 1) + jj
            conv = conv + convw_ref[jj:jj + 1, cs] * xbc_buf[cb, off:off + T, :]
        conv_blocks.append(conv * jax.nn.sigmoid(conv))
        xbc_buf[cb, 0:CONV_HIST, :] = xbc_buf[cb, T:T + CONV_HIST, :]
    xbc = jnp.concatenate(conv_blocks, axis=1)
    xs = xbc[:, :D_MODEL]
    bmat = xbc[:, D_MODEL:D_MODEL + SSD_N_GROUPS * SSD_D_STATE].astype(BF16)
    cmat = xbc[:, D_MODEL + SSD_N_GROUPS * SSD_D_STATE:].astype(BF16)

    dt = jax.nn.softplus(proj(C_DT, C_END) + dtb_ref[...])
    da = dt * (-jnp.exp(alog_ref[...]))
    row_i = lax.broadcasted_iota(jnp.int32, (T, T), 0)
    col_i = lax.broadcasted_iota(jnp.int32, (T, T), 1)
    causal = row_i >= col_i
    da_hi = da.astype(BF16)
    da_r1 = da - da_hi.astype(F32)
    da_mid = da_r1.astype(BF16)
    da_lo = (da_r1 - da_mid.astype(F32)).astype(BF16)
    tril = jnp.where(causal, 1.0, 0.0).astype(BF16)
    acs3 = _dot(tril, jnp.concatenate([da_hi, da_mid, da_lo], axis=1))
    acs = acs3[:, :LANES] + acs3[:, LANES:2 * LANES] + acs3[:, 2 * LANES:]
    acs_t = acs.T

    def head_cols(arr):
        return [jnp.broadcast_to(arr[:, h:h + 1], (T, LANES)) for h in range(SSD_N_HEADS)]

    def per_channel(cols):
        return jnp.concatenate([jnp.where(lane_lo, cols[2 * p], cols[2 * p + 1])
                                for p in range(SSD_N_HEADS // 2)], axis=1)

    acs_cols = head_cols(acs)
    acs_ch = per_channel(acs_cols)
    dt_ch = per_channel(head_cols(dt))
    xdt = xs * dt_ch
    last = acs_ch[T - 1:T, :]
    grow = jnp.exp(acs_ch)
    tail = jnp.exp(last - acs_ch)
    tile_decay = jnp.exp(last)
    xdt_b = xdt.astype(BF16)
    xdt_tail_b = (xdt * tail).astype(BF16)

    y_tiles = []
    for g in range(SSD_N_GROUPS):
        gs = slice(g * SSD_GROUP_WIDTH, (g + 1) * SSD_GROUP_WIDTH)
        bm_g = bmat[:, g * SSD_D_STATE:(g + 1) * SSD_D_STATE]
        cm_g = cmat[:, g * SSD_D_STATE:(g + 1) * SSD_D_STATE]
        cb = _dot_nt(cm_g, bm_g)
        state = state_ref[g]
        y_off = _dot(cm_g, state.astype(BF16)) * grow[:, gs]
        state_ref[g] = state * tile_decay[:, gs] + _dot_tn(bm_g, xdt_tail_b[:, gs])
        for pp in range(SSD_GROUP_WIDTH // LANES):
            p = g * (SSD_GROUP_WIDTH // LANES) + pp
            blk = xdt_b[:, p * LANES:(p + 1) * LANES]
            y_pair = y_off[:, pp * LANES:(pp + 1) * LANES]
            for half in range(2):
                h = 2 * p + half
                col = acs_cols[h]
                seg = jnp.concatenate([col] * (T // LANES), axis=1) - acs_t[h:h + 1, :]
                lmat = jnp.exp(jnp.where(causal, seg, -jnp.inf))
                m_h = (cb * lmat).astype(BF16)
                keep = lane_lo if half == 0 else jnp.logical_not(lane_lo)
                y_pair = y_pair + _dot(m_h, jnp.where(keep, blk, jnp.zeros_like(blk)))
            y_tiles.append(y_pair)
    y = jnp.concatenate(y_tiles, axis=1) + xs * dskip_ref[...]
    z = proj(C_Z, C_XBC)
    y = y * (z * jax.nn.sigmoid(z))
    y_norm = []
    for g in range(SSD_N_GROUPS):
        yg = y[:, g * SSD_GROUP_WIDTH:(g + 1) * SSD_GROUP_WIDTH]
        y_norm.append(yg * lax.rsqrt(jnp.mean(yg * yg, axis=-1, keepdims=True) + EPS))
    y_ssd = (jnp.concatenate(y_norm, axis=1) * ssdnw_ref[...]).astype(BF16)

    q = (proj(C_Q, C_K) * (ATT_HEAD_DIM ** -0.5)).astype(BF16)
    kv = proj(C_K, C_U).astype(BF16)
    k_buf[ATT_PAD:ATT_PAD + T, :] = kv[:, :LANES]
    v_buf[ATT_PAD:ATT_PAD + T, :] = kv[:, LANES:]
    key_i = lax.broadcasted_iota(jnp.int32, (1, ATT_BAND), 1)
    rep = ATT_N_HEADS // ATT_N_KV
    sink_rows = jnp.concatenate([jnp.full((ATT_CHUNK, LANES), sink_ref[hq], F32)
                                 for hq in range(ATT_N_HEADS)], axis=0)
    att_rows = []
    for c in range(T // ATT_CHUNK):
        r0 = c * ATT_CHUNK
        kband = k_buf[r0:r0 + ATT_BAND, :]
        vband = v_buf[r0:r0 + ATT_BAND, :]
        valid = (key_i + (r0 - ATT_PAD) + t0) >= 0
        q_stack = []
        for hq in range(ATT_N_HEADS):
            qblk = q[r0:r0 + ATT_CHUNK, (hq % rep) * LANES:(hq % rep + 1) * LANES]
            keep = lane_lo if hq < rep else jnp.logical_not(lane_lo)
            q_stack.append(jnp.where(keep, qblk, jnp.zeros_like(qblk)))
        s = _dot_nt(jnp.concatenate(q_stack, axis=0), kband) + bias_ref[...]
        s = jnp.where(valid, s, -jnp.inf)
        m = jnp.maximum(jnp.max(s, axis=-1, keepdims=True), sink_rows)
        pr = jnp.exp(s - jnp.concatenate([m, m[:, :ATT_BAND - LANES]], axis=1))
        den = jnp.sum(pr, axis=-1, keepdims=True) + jnp.exp(sink_rows - m)
        o = _dot(pr.astype(BF16), vband) * (1.0 / den)
        att_rows.append(jnp.concatenate(
            [jnp.where(lane_lo, o[jb * ATT_CHUNK:(jb + 1) * ATT_CHUNK],
                       o[(jb + rep) * ATT_CHUNK:(jb + rep + 1) * ATT_CHUNK]) for jb in range(rep)], axis=1))
    y_att = jnp.concatenate(att_rows, axis=0).astype(BF16)

    u = proj(C_U, C_DT)
    tpos = t0 + lax.broadcasted_iota(jnp.int32, (T, 1), 0)
    pool_tiles = []
    for gi, w in enumerate(POOL_WINDOWS):
        ug = u[:, gi * POOL_GROUP:(gi + 1) * POOL_GROUP]
        u_buf[gi, POOL_HIST:POOL_HIST + T, :] = ug
        acc = ug
        for back in range(1, w):
            acc = acc + u_buf[gi, POOL_HIST - back:POOL_HIST - back + T, :]
        u_buf[gi, 0:POOL_HIST, :] = u_buf[gi, T:T + POOL_HIST, :]
        cnt = jnp.minimum(tpos + 1, w).astype(F32)
        pooled = acc / cnt - ug
        pool_tiles.append(_dot(pooled.astype(BF16), poolw_ref[gi]))
    y_pool = (jnp.concatenate(pool_tiles, axis=1) * pools_ref[...]).astype(BF16)

    merged = gate_ssd * _dot(y_ssd, wus_ref[...])
    merged = merged + gate_att * _dot(y_att, wua_ref[...])
    merged = merged + gate_pool * _dot(y_pool, wup_ref[...])
    o_ref[...] = x + _dot(merged.astype(BF16), wout_ref[...])

    k_buf[0:ATT_PAD, :] = k_buf[T:T + ATT_PAD, :]
    v_buf[0:ATT_PAD, :] = v_buf[T:T + ATT_PAD, :]


def _mlp_kernel(x_ref, n2_ref, w1_ref, w2_ref, nf_ref, o_ref, *, final_norm):
    x = x_ref[...]
    hb = _rmsnorm(x, n2_ref[...]).astype(BF16)
    a = jnp.maximum(_dot(hb, w1_ref[...]), 0.0)
    out = x + _dot((a * a).astype(BF16), w2_ref[...])
    if final_norm:
        out = _rmsnorm(out, nf_ref[...])
    o_ref[...] = out


def _resident(shape, index):
    return pl.BlockSpec(shape, index, pipeline_mode=pl.Buffered(1))


def _mixer_call(x, layer, p, tile):
    b, s, _ = x.shape
    assert s % tile == 0 and tile % LANES == 0 and tile >= ATT_PAD
    l3 = lambda i, j: (layer, 0, 0)
    l4 = lambda i, j: (layer, 0, 0, 0)
    vec = lambda n: _resident((None, 1, n), l3)
    in_specs = [
        pl.BlockSpec((None, tile, D_MODEL), lambda i, j: (i, j, 0)),
        vec(D_MODEL),
        _resident((None, D_MODEL, C_END), l3),
        _resident((None, SSD_CONV, SSD_CONV_DIM), l3),
        vec(SSD_CONV_DIM),
        vec(LANES),
        vec(LANES),
        vec(D_MODEL),
        vec(D_MODEL),
        pl.BlockSpec(memory_space=pltpu.SMEM),
        _resident((ATT_N_HEADS * ATT_CHUNK, ATT_BAND), lambda i, j: (0, 0)),
        _resident((None, len(POOL_WINDOWS), POOL_GROUP, POOL_GROUP), l4),
        vec(POOL_WIDTH),
        _resident((None, D_MODEL, D_MODEL), l3),
        _resident((None, ATT_N_HEADS * ATT_HEAD_DIM, D_MODEL), l3),
        _resident((None, POOL_WIDTH, D_MODEL), l3),
        _resident((None, D_MODEL, D_MODEL), l3),
    ]
    return pl.pallas_call(
        functools.partial(_mixer_kernel, tile=tile),
        grid=(b, s // tile),
        in_specs=in_specs,
        out_specs=pl.BlockSpec((None, tile, D_MODEL), lambda i, j: (i, j, 0)),
        out_shape=jax.ShapeDtypeStruct(x.shape, F32),
        scratch_shapes=[
            pltpu.VMEM((SSD_CONV_DIM // LANES, CONV_HIST + tile, LANES), F32),
            pltpu.VMEM((ATT_PAD + tile, LANES), BF16),
            pltpu.VMEM((ATT_PAD + tile, LANES), BF16),
            pltpu.VMEM((POOL_WIDTH // LANES, POOL_HIST + tile, LANES), F32),
            pltpu.VMEM((SSD_N_GROUPS, SSD_D_STATE, SSD_GROUP_WIDTH), F32),
        ],
        compiler_params=pltpu.CompilerParams(
            dimension_semantics=("arbitrary", "arbitrary"), vmem_limit_bytes=VMEM_LIMIT),
        name="mixer",
    )(x, p["norm1_w"], p["w_in"], p["conv_w"], p["conv_b"], p["dt_bias"], p["a_log"], p["d_skip"],
      p["ssd_norm_w"], p["sinks"][layer], p["att_bias"], p["pool_w"], p["pool_scale"],
      p["w_up_ssd"], p["w_up_attn"], p["w_up_pool"], p["w_out"])


def _mlp_call(x2d, layer, p, tile, final_norm):
    n, _ = x2d.shape
    assert n % tile == 0
    l3 = lambda i: (layer, 0, 0)
    return pl.pallas_call(
        functools.partial(_mlp_kernel, final_norm=final_norm),
        grid=(n // tile,),
        in_specs=[
            pl.BlockSpec((tile, D_MODEL), lambda i: (i, 0)),
            _resident((None, 1, D_MODEL), l3),
            _resident((None, D_MODEL, D_FF), l3),
            _resident((None, D_FF, D_MODEL), l3),
            _resident((1, D_MODEL), lambda i: (0, 0)),
        ],
        out_specs=pl.BlockSpec((tile, D_MODEL), lambda i: (i, 0)),
        out_shape=jax.ShapeDtypeStruct(x2d.shape, F32),
        compiler_params=pltpu.CompilerParams(
            dimension_semantics=("arbitrary",), vmem_limit_bytes=VMEM_LIMIT),
        name="mlp",
    )(x2d, p["norm2_w"], p["w_mlp_in"], p["w_mlp_out"], p["final_norm_w"])


def _prepare_params(norm1_w, w_in, ssd_conv_w, ssd_conv_b, ssd_dt_bias, ssd_a_log, ssd_d, ssd_norm_w,
                    attn_sinks, rel_bias, pool_w, pool_scale, w_up_ssd, w_up_attn, w_up_pool, w_out,
                    norm2_w, w_mlp_in, w_mlp_out, final_norm_w):
    depth = w_in.shape[0]
    perm = _att_head_perm()
    seg = lambda a, b: w_in[:, :, a:b]
    pad_lanes = lambda a: jnp.pad(a, ((0, 0),) * (a.ndim - 1) + ((0, LANES - a.shape[-1]),))
    w_in_k = jnp.concatenate([
        seg(0, _GATE_END), seg(_GATE_END, _Z_END), seg(_Z_END, _XBC_END),
        seg(_DT_END, _Q_END)[:, :, perm], seg(_Q_END, _K_END), seg(_K_END, _V_END),
        seg(_V_END, _IN_COLS), pad_lanes(seg(_XBC_END, _DT_END))], axis=-1).astype(BF16)
    row = lambda a: a.astype(F32)[:, None, :]
    att_bias = jnp.moveaxis(rel_bias.astype(F32)[_t5_bucket_table()], -1, 0)
    att_bias = att_bias.reshape(ATT_N_HEADS * ATT_CHUNK, ATT_BAND)
    return {
        "norm1_w": row(norm1_w), "w_in": w_in_k,
        "conv_w": ssd_conv_w.astype(F32), "conv_b": row(ssd_conv_b),
        "dt_bias": row(pad_lanes(ssd_dt_bias)), "a_log": row(pad_lanes(ssd_a_log)),
        "d_skip": row(jnp.repeat(ssd_d, SSD_HEAD_DIM, axis=-1)), "ssd_norm_w": row(ssd_norm_w),
        "sinks": attn_sinks.astype(F32), "att_bias": att_bias,
        "pool_w": pool_w.astype(BF16), "pool_scale": row(pool_scale),
        "w_up_ssd": w_up_ssd.astype(BF16), "w_up_attn": w_up_attn[:, perm, :].astype(BF16),
        "w_up_pool": w_up_pool.astype(BF16), "w_out": w_out.astype(BF16),
        "norm2_w": row(norm2_w), "w_mlp_in": w_mlp_in.astype(BF16), "w_mlp_out": w_mlp_out.astype(BF16),
        "final_norm_w": final_norm_w.astype(F32)[None, :], "depth": depth,
    }


def kernel(x, norm1_w, w_in, ssd_conv_w, ssd_conv_b, ssd_dt_bias, ssd_a_log, ssd_d, ssd_norm_w, attn_sinks, rel_bias, pool_w, pool_scale, w_up_ssd, w_up_attn, w_up_pool, w_out, norm2_w, w_mlp_in, w_mlp_out, final_norm_w):
    p = _prepare_params(norm1_w, w_in, ssd_conv_w, ssd_conv_b, ssd_dt_bias, ssd_a_log, ssd_d, ssd_norm_w,
                        attn_sinks, rel_bias, pool_w, pool_scale, w_up_ssd, w_up_attn, w_up_pool, w_out,
                        norm2_w, w_mlp_in, w_mlp_out, final_norm_w)
    b, s, d = x.shape
    seq_tile = min(SEQ_TILE, s)
    mlp_tile = min(MLP_TILE, b * s)
    depth = p.pop("depth")
    x = x.astype(F32)
    for layer in range(depth):
        x = _mixer_call(x, layer, p, seq_tile)
        x = _mlp_call(x.reshape(b * s, d), layer, p, mlp_tile, layer == depth - 1).reshape(b, s, d)
    return x
```

```python
import functools
import math

import numpy as np
import jax
import jax.numpy as jnp
from jax import lax
from jax.experimental import pallas as pl
from jax.experimental.pallas import tpu as pltpu

F32 = jnp.float32
BF16 = jnp.bfloat16

D_MODEL = 1024
EPS = 1e-6

SSD_HEAD_DIM = 64
SSD_N_HEADS = 16
SSD_N_GROUPS = 2
SSD_D_STATE = 128
SSD_CONV = 4
SSD_GROUP_WIDTH = D_MODEL // SSD_N_GROUPS
SSD_CONV_DIM = D_MODEL + 2 * SSD_N_GROUPS * SSD_D_STATE

ATT_HEAD_DIM = 64
ATT_N_HEADS = 8
ATT_N_KV = 2
ATT_CHUNK = 64
ATT_PAD = 128
ATT_BAND = ATT_PAD + ATT_CHUNK
N_BUCKETS = 32
MAX_DISTANCE = 128

POOL_WIDTH = 512
POOL_WINDOWS = (2, 4, 8, 16)
POOL_GROUP = 128
POOL_HIST = 16
CONV_HIST = 8

D_FF = 4 * D_MODEL

LANES = 128

_GATE_END = 3 * D_MODEL
_Z_END = _GATE_END + D_MODEL
_XBC_END = _Z_END + SSD_CONV_DIM
_DT_END = _XBC_END + SSD_N_HEADS
_Q_END = _DT_END + ATT_N_HEADS * ATT_HEAD_DIM
_K_END = _Q_END + ATT_N_KV * ATT_HEAD_DIM
_V_END = _K_END + ATT_N_KV * ATT_HEAD_DIM
_IN_COLS = _V_END + POOL_WIDTH

B_GATE = 0
B_Z = B_GATE + 3 * D_MODEL
B_Q = B_Z + D_MODEL
B_K = B_Q + ATT_N_HEADS * ATT_HEAD_DIM
B_V = B_K + LANES
PB_COLS = B_V + LANES
F_XBC = 0
F_U = F_XBC + SSD_CONV_DIM
F_DT = F_U + POOL_WIDTH
PF_COLS = F_DT + LANES
C_END = PB_COLS + PF_COLS
PB_SEGMENTS = ((B_GATE, B_GATE + D_MODEL), (B_GATE + D_MODEL, B_GATE + 2 * D_MODEL),
               (B_GATE + 2 * D_MODEL, B_Z), (B_Z, B_Q), (B_Q, PB_COLS))
PF_SEGMENTS = ((F_XBC, F_U), (F_U, PF_COLS))

SEQ_TILE = 256
MLP_TILE = 512
VMEM_LIMIT = 60 * 1024 * 1024


def _att_head_perm():
    cols = np.arange(ATT_N_HEADS * ATT_HEAD_DIM)
    tile, half, d = cols // LANES, (cols % LANES) // ATT_HEAD_DIM, cols % ATT_HEAD_DIM
    return (tile + (ATT_N_HEADS // ATT_N_KV) * half) * ATT_HEAD_DIM + d


def _t5_bucket_table():
    nb = N_BUCKETS // 2
    qpos = jnp.arange(ATT_CHUNK, dtype=jnp.int32)
    kpos = jnp.arange(ATT_BAND, dtype=jnp.int32) - ATT_PAD
    rel = kpos[None, :] - qpos[:, None]
    out = (rel > 0).astype(jnp.int32) * nb
    n = jnp.abs(rel)
    max_exact = nb // 2
    nf = jnp.maximum(n, 1).astype(jnp.float32)
    large = max_exact + (jnp.log(nf / max_exact) / math.log(MAX_DISTANCE / max_exact)
                         * (nb - max_exact)).astype(jnp.int32)
    large = jnp.minimum(large, nb - 1)
    return out + jnp.where(n < max_exact, n, large)


def _rmsnorm(x, w):
    return x * lax.rsqrt(jnp.mean(x * x, axis=-1, keepdims=True) + EPS) * w


def _dot(a, b):
    return jnp.dot(a, b, preferred_element_type=F32)


def _dot_nt(a, b):
    return lax.dot_general(a, b, (((1,), (1,)), ((), ())), preferred_element_type=F32)


def _dot_tn(a, b):
    return lax.dot_general(a, b, (((0,), (0,)), ((), ())), preferred_element_type=F32)


def _mixer_kernel(x2_ref, h2_ref, hn_ref, win_ref, convw_ref, convb_ref, dtb_ref, alog_ref, dskip_ref,
                  ssdnw_ref, sink_ref, bias_ref, poolw_ref, pools_ref, wus_ref, wua_ref, wup_ref,
                  wout_ref, o_ref, pb0, pf0, pb1, pf1, xbc_buf, k_buf, v_buf, u_buf, state_ref,
                  *, tile, tiles_per_row):
    T = tile
    g = pl.program_id(0)
    tile_in_row = lax.rem(2 * g, tiles_per_row)

    def projection_steps(h_ref, rows, pb, pf):
        def park_b(a, b):
            pb[:, a:b] = _dot(h_ref[rows, :], win_ref[:, a:b]).astype(BF16)

        def park_f(a, b):
            pf[:, a:b] = _dot(h_ref[rows, :], win_ref[:, PB_COLS + a:PB_COLS + b])

        return ([functools.partial(park_b, a, b) for a, b in PB_SEGMENTS]
                + [functools.partial(park_f, a, b) for a, b in PF_SEGMENTS])

    first, second = slice(0, T), slice(T, 2 * T)

    @pl.when(g == 0)
    def _():
        for step in projection_steps(h2_ref, first, pb0, pf0):
            step()

    @pl.when(tile_in_row == 0)
    def _():
        xbc_buf[:, 0:CONV_HIST, :] = jnp.zeros((SSD_CONV_DIM // LANES, CONV_HIST, LANES), F32)
        k_buf[0:ATT_PAD, :] = jnp.zeros((ATT_PAD, LANES), BF16)
        v_buf[0:ATT_PAD, :] = jnp.zeros((ATT_PAD, LANES), BF16)
        u_buf[:, 0:POOL_HIST, :] = jnp.zeros((POOL_WIDTH // LANES, POOL_HIST, LANES), F32)
        state_ref[...] = jnp.zeros(state_ref.shape, F32)

    lane_lo = lax.broadcasted_iota(jnp.int32, (1, LANES), 1) < (LANES // 2)
    row_i = lax.broadcasted_iota(jnp.int32, (T, T), 0)
    col_i = lax.broadcasted_iota(jnp.int32, (T, T), 1)
    causal = row_i >= col_i
    tril = jnp.where(causal, 1.0, 0.0).astype(BF16)

    ISSUE_POINTS = 10

    def mix(pb, pf, rows, t0, ahead):
        ahead = list(ahead)
        points_left = [ISSUE_POINTS]

        def issue():
            for _ in range(-(-len(ahead) // points_left[0])):
                ahead.pop(0)()
            points_left[0] -= 1

        issue()
        conv_blocks = []
        for cb in range(SSD_CONV_DIM // LANES):
            cs = slice(cb * LANES, (cb + 1) * LANES)
            xbc_buf[cb, CONV_HIST:CONV_HIST + T, :] = pf[:, F_XBC + cb * LANES:F_XBC + (cb + 1) * LANES]
            conv = convb_ref[:, cs]
            for jj in range(SSD_CONV):
                off = CONV_HIST - (SSD_CONV - 1) + jj
                conv = conv + convw_ref[jj:jj + 1, cs] * xbc_buf[cb, off:off + T, :]
            conv_blocks.append(conv * jax.nn.sigmoid(conv))
            xbc_buf[cb, 0:CONV_HIST, :] = xbc_buf[cb, T:T + CONV_HIST, :]
        xbc = jnp.concatenate(conv_blocks, axis=1)
        xs = xbc[:, :D_MODEL]
        bmat = xbc[:, D_MODEL:D_MODEL + SSD_N_GROUPS * SSD_D_STATE].astype(BF16)
        cmat = xbc[:, D_MODEL + SSD_N_GROUPS * SSD_D_STATE:].astype(BF16)
        issue()

        dt = jax.nn.softplus(pf[:, F_DT:PF_COLS] + dtb_ref[...])
        da = dt * (-jnp.exp(alog_ref[...]))
        da_hi = da.astype(BF16)
        da_r1 = da - da_hi.astype(F32)
        da_mid = da_r1.astype(BF16)
        da_lo = (da_r1 - da_mid.astype(F32)).astype(BF16)
        acs3 = _dot(tril, jnp.concatenate([da_hi, da_mid, da_lo], axis=1))
        acs = acs3[:, :LANES] + acs3[:, LANES:2 * LANES] + acs3[:, 2 * LANES:]
        acs_t = acs.T

        def head_cols(arr):
            return [jnp.broadcast_to(arr[:, h:h + 1], (T, LANES)) for h in range(SSD_N_HEADS)]

        def per_channel(cols):
            return jnp.concatenate([jnp.where(lane_lo, cols[2 * p], cols[2 * p + 1])
                                    for p in range(SSD_N_HEADS // 2)], axis=1)

        acs_cols = head_cols(acs)
        acs_ch = per_channel(acs_cols)
        dt_ch = per_channel(head_cols(dt))
        xdt = xs * dt_ch
        last = acs_ch[T - 1:T, :]
        grow = jnp.exp(acs_ch)
        tail = jnp.exp(last - acs_ch)
        tile_decay = jnp.exp(last)
        xdt_b = xdt.astype(BF16)
        xdt_tail_b = (xdt * tail).astype(BF16)
        issue()

        y_tiles = []
        for grp in range(SSD_N_GROUPS):
            gs = slice(grp * SSD_GROUP_WIDTH, (grp + 1) * SSD_GROUP_WIDTH)
            bm_g = bmat[:, grp * SSD_D_STATE:(grp + 1) * SSD_D_STATE]
            cm_g = cmat[:, grp * SSD_D_STATE:(grp + 1) * SSD_D_STATE]
            cb = _dot_nt(cm_g, bm_g)
            state = state_ref[grp]
            y_off = _dot(cm_g, state.astype(BF16)) * grow[:, gs]
            state_ref[grp] = state * tile_decay[:, gs] + _dot_tn(bm_g, xdt_tail_b[:, gs])
            for pp in range(SSD_GROUP_WIDTH // LANES):
                p = grp * (SSD_GROUP_WIDTH // LANES) + pp
                blk = xdt_b[:, p * LANES:(p + 1) * LANES]
                y_pair = y_off[:, pp * LANES:(pp + 1) * LANES]
                for half in range(2):
                    h = 2 * p + half
                    col = acs_cols[h]
                    seg = jnp.concatenate([col] * (T // LANES), axis=1) - acs_t[h:h + 1, :]
                    lmat = jnp.exp(jnp.where(causal, seg, -jnp.inf))
                    m_h = (cb * lmat).astype(BF16)
                    keep = lane_lo if half == 0 else jnp.logical_not(lane_lo)
                    y_pair = y_pair + _dot(m_h, jnp.where(keep, blk, jnp.zeros_like(blk)))
                y_tiles.append(y_pair)
            issue()
        y = jnp.concatenate(y_tiles, axis=1) + xs * dskip_ref[...]
        z = pb[:, B_Z:B_Q].astype(F32)
        y = y * (z * jax.nn.sigmoid(z))
        y_norm = []
        for grp in range(SSD_N_GROUPS):
            yg = y[:, grp * SSD_GROUP_WIDTH:(grp + 1) * SSD_GROUP_WIDTH]
            y_norm.append(yg * lax.rsqrt(jnp.mean(yg * yg, axis=-1, keepdims=True) + EPS))
        y_ssd = (jnp.concatenate(y_norm, axis=1) * ssdnw_ref[...]).astype(BF16)
        issue()

        q = pb[:, B_Q:B_K] * (ATT_HEAD_DIM ** -0.5)
        k_buf[ATT_PAD:ATT_PAD + T, :] = pb[:, B_K:B_V]
        v_buf[ATT_PAD:ATT_PAD + T, :] = pb[:, B_V:PB_COLS]
        key_i = lax.broadcasted_iota(jnp.int32, (1, ATT_BAND), 1)
        rep = ATT_N_HEADS // ATT_N_KV
        sink_rows = jnp.concatenate([jnp.full((ATT_CHUNK, LANES), sink_ref[hq], F32)
                                     for hq in range(ATT_N_HEADS)], axis=0)
        att_rows = []
        for c in range(T // ATT_CHUNK):
            r0 = c * ATT_CHUNK
            kband = k_buf[r0:r0 + ATT_BAND, :]
            vband = v_buf[r0:r0 + ATT_BAND, :]
            valid = (key_i + (r0 - ATT_PAD) + t0) >= 0
            q_stack = []
            for hq in range(ATT_N_HEADS):
                qblk = q[r0:r0 + ATT_CHUNK, (hq % rep) * LANES:(hq % rep + 1) * LANES]
                keep = lane_lo if hq < rep else jnp.logical_not(lane_lo)
                q_stack.append(jnp.where(keep, qblk, jnp.zeros_like(qblk)))
            s = _dot_nt(jnp.concatenate(q_stack, axis=0), kband) + bias_ref[...]
            s = jnp.where(valid, s, -jnp.inf)
            m = jnp.maximum(jnp.max(s, axis=-1, keepdims=True), sink_rows)
            pr = jnp.exp(s - jnp.concatenate([m, m[:, :ATT_BAND - LANES]], axis=1))
            den = jnp.sum(pr, axis=-1, keepdims=True) + jnp.exp(sink_rows - m)
            o = _dot(pr.astype(BF16), vband) * (1.0 / den)
            att_rows.append(jnp.concatenate(
                [jnp.where(lane_lo, o[jb * ATT_CHUNK:(jb + 1) * ATT_CHUNK],
                           o[(jb + rep) * ATT_CHUNK:(jb + rep + 1) * ATT_CHUNK]) for jb in range(rep)], axis=1))
            issue()
        y_att = jnp.concatenate(att_rows, axis=0).astype(BF16)
        k_buf[0:ATT_PAD, :] = k_buf[T:T + ATT_PAD, :]
        v_buf[0:ATT_PAD, :] = v_buf[T:T + ATT_PAD, :]

        tpos = t0 + lax.broadcasted_iota(jnp.int32, (T, 1), 0)
        pool_tiles = []
        for gi, w in enumerate(POOL_WINDOWS):
            ug = pf[:, F_U + gi * POOL_GROUP:F_U + (gi + 1) * POOL_GROUP]
            u_buf[gi, POOL_HIST:POOL_HIST + T, :] = ug
            acc = ug
            for back in range(1, w):
                acc = acc + u_buf[gi, POOL_HIST - back:POOL_HIST - back + T, :]
            u_buf[gi, 0:POOL_HIST, :] = u_buf[gi, T:T + POOL_HIST, :]
            cnt = jnp.minimum(tpos + 1, w).astype(F32)
            pooled = acc / cnt - ug
            pool_tiles.append(_dot(pooled.astype(BF16), poolw_ref[gi]))
        y_pool = (jnp.concatenate(pool_tiles, axis=1) * pools_ref[...]).astype(BF16)
        assert points_left[0] == 0 and not ahead

        acc = []

        def branch(i, y_branch, w_ref):
            gate = jax.nn.sigmoid(pb[:, B_GATE + i * D_MODEL:B_GATE + (i + 1) * D_MODEL].astype(F32))
            acc.append(gate * _dot(y_branch, w_ref[...]))

        def finish():
            merged = acc[0] + acc[1] + acc[2]
            o_ref[rows, :] = x2_ref[rows, :] + _dot(merged.astype(BF16), wout_ref[...])

        return [functools.partial(branch, 0, y_ssd, wus_ref), functools.partial(branch, 1, y_att, wua_ref),
                functools.partial(branch, 2, y_pool, wup_ref), finish]

    merge_first = mix(pb0, pf0, first, tile_in_row * T, projection_steps(h2_ref, second, pb1, pf1))
    proj_next = projection_steps(hn_ref, slice(None), pb0, pf0)
    park_gate, park_rest, park_f = proj_next[0:3], proj_next[3:5], proj_next[5:7]
    ahead = list(park_f)
    for branch_step, gate_step in zip(merge_first[0:3], park_gate):
        ahead += [branch_step, gate_step]
    ahead += park_rest + merge_first[3:4]
    for step in mix(pb1, pf1, second, (tile_in_row + 1) * T, ahead):
        step()


def _mlp_kernel(x_ref, n2_ref, w1_ref, w2_ref, nnext_ref, o_ref, *maybe_hn_ref, last):
    x = x_ref[...]
    hb = _rmsnorm(x, n2_ref[...]).astype(BF16)
    a = jnp.maximum(_dot(hb, w1_ref[...]), 0.0)
    out = x + _dot((a * a).astype(BF16), w2_ref[...])
    if last:
        o_ref[...] = _rmsnorm(out, nnext_ref[...])
    else:
        o_ref[...] = out
        maybe_hn_ref[0][...] = _rmsnorm(out, nnext_ref[...]).astype(BF16)


def _norm_kernel(x_ref, w_ref, o_ref):
    o_ref[...] = _rmsnorm(x_ref[...], w_ref[...]).astype(BF16)


def _resident(shape, index):
    return pl.BlockSpec(shape, index, pipeline_mode=pl.Buffered(1))


def _mixer_call(x, xn, layer, p, tile):
    b, s, _ = x.shape
    tiles_per_row = s // tile
    assert s % tile == 0 and tiles_per_row % 2 == 0 and tile % LANES == 0 and tile >= ATT_PAD
    n_tiles = b * tiles_per_row
    l3 = lambda g: (layer, 0, 0)
    l4 = lambda g: (layer, 0, 0, 0)
    vec = lambda n: _resident((None, 1, n), l3)
    in_specs = [
        pl.BlockSpec((None, 2 * tile, D_MODEL), lambda g: (g, 0, 0)),
        pl.BlockSpec((None, 2 * tile, D_MODEL), lambda g: (g, 0, 0)),
        pl.BlockSpec((None, tile, D_MODEL), lambda g: (jnp.minimum(2 * g + 2, n_tiles - 1), 0, 0)),
        _resident((None, D_MODEL, C_END), l3),
        _resident((None, SSD_CONV, SSD_CONV_DIM), l3),
        vec(SSD_CONV_DIM),
        vec(LANES),
        vec(LANES),
        vec(D_MODEL),
        vec(D_MODEL),
        pl.BlockSpec(memory_space=pltpu.SMEM),
        _resident((ATT_N_HEADS * ATT_CHUNK, ATT_BAND), lambda g: (0, 0)),
        _resident((None, len(POOL_WINDOWS), POOL_GROUP, POOL_GROUP), l4),
        vec(POOL_WIDTH),
        _resident((None, D_MODEL, D_MODEL), l3),
        _resident((None, ATT_N_HEADS * ATT_HEAD_DIM, D_MODEL), l3),
        _resident((None, POOL_WIDTH, D_MODEL), l3),
        _resident((None, D_MODEL, D_MODEL), l3),
    ]
    out = pl.pallas_call(
        functools.partial(_mixer_kernel, tile=tile, tiles_per_row=tiles_per_row),
        grid=(n_tiles // 2,),
        in_specs=in_specs,
        out_specs=pl.BlockSpec((None, 2 * tile, D_MODEL), lambda g: (g, 0, 0)),
        out_shape=jax.ShapeDtypeStruct((n_tiles // 2, 2 * tile, D_MODEL), F32),
        scratch_shapes=[
            pltpu.VMEM((tile, PB_COLS), BF16), pltpu.VMEM((tile, PF_COLS), F32),
            pltpu.VMEM((tile, PB_COLS), BF16), pltpu.VMEM((tile, PF_COLS), F32),
            pltpu.VMEM((SSD_CONV_DIM // LANES, CONV_HIST + tile, LANES), F32),
            pltpu.VMEM((ATT_PAD + tile, LANES), BF16),
            pltpu.VMEM((ATT_PAD + tile, LANES), BF16),
            pltpu.VMEM((POOL_WIDTH // LANES, POOL_HIST + tile, LANES), F32),
            pltpu.VMEM((SSD_N_GROUPS, SSD_D_STATE, SSD_GROUP_WIDTH), F32),
        ],
        compiler_params=pltpu.CompilerParams(
            dimension_semantics=("arbitrary",), vmem_limit_bytes=VMEM_LIMIT),
        name="mixer",
    )(x.reshape(n_tiles // 2, 2 * tile, D_MODEL), xn.reshape(n_tiles // 2, 2 * tile, D_MODEL),
      xn.reshape(n_tiles, tile, D_MODEL), p["w_in"], p["conv_w"], p["conv_b"], p["dt_bias"], p["a_log"], p["d_skip"],
      p["ssd_norm_w"], p["sinks"][layer], p["att_bias"], p["pool_w"], p["pool_scale"],
      p["w_up_ssd"], p["w_up_attn"], p["w_up_pool"], p["w_out"])
    return out.reshape(b, s, D_MODEL)


def _mlp_call(x2d, layer, p, tile, last):
    n, _ = x2d.shape
    assert n % tile == 0
    rows = pl.BlockSpec((tile, D_MODEL), lambda i: (i, 0))
    if last:
        next_norm = _resident((1, D_MODEL), lambda i: (0, 0))
        next_w, out_specs, out_shape = p["final_norm_w"], rows, jax.ShapeDtypeStruct(x2d.shape, F32)
    else:
        next_norm = _resident((None, 1, D_MODEL), lambda i: (layer + 1, 0, 0))
        next_w, out_specs = p["norm1_w"], (rows, rows)
        out_shape = (jax.ShapeDtypeStruct(x2d.shape, F32), jax.ShapeDtypeStruct(x2d.shape, BF16))
    l3 = lambda i: (layer, 0, 0)
    return pl.pallas_call(
        functools.partial(_mlp_kernel, last=last),
        grid=(n // tile,),
        in_specs=[rows, _resident((None, 1, D_MODEL), l3), _resident((None, D_MODEL, D_FF), l3),
                  _resident((None, D_FF, D_MODEL), l3), next_norm],
        out_specs=out_specs,
        out_shape=out_shape,
        compiler_params=pltpu.CompilerParams(
            dimension_semantics=("arbitrary",), vmem_limit_bytes=VMEM_LIMIT),
        name="mlp",
    )(x2d, p["norm2_w"], p["w_mlp_in"], p["w_mlp_out"], next_w)


def _norm_call(x2d, p, tile):
    n, _ = x2d.shape
    rows = pl.BlockSpec((tile, D_MODEL), lambda i: (i, 0))
    return pl.pallas_call(
        _norm_kernel, grid=(n // tile,),
        in_specs=[rows, _resident((None, 1, D_MODEL), lambda i: (0, 0, 0))],
        out_specs=rows, out_shape=jax.ShapeDtypeStruct(x2d.shape, BF16), name="norm",
    )(x2d, p["norm1_w"])


def _prepare_params(norm1_w, w_in, ssd_conv_w, ssd_conv_b, ssd_dt_bias, ssd_a_log, ssd_d, ssd_norm_w,
                    attn_sinks, rel_bias, pool_w, pool_scale, w_up_ssd, w_up_attn, w_up_pool, w_out,
                    norm2_w, w_mlp_in, w_mlp_out, final_norm_w):
    depth = w_in.shape[0]
    perm = _att_head_perm()
    seg = lambda a, b: w_in[:, :, a:b]
    pad_lanes = lambda a: jnp.pad(a, ((0, 0),) * (a.ndim - 1) + ((0, LANES - a.shape[-1]),))
    w_in_k = jnp.concatenate([
        seg(0, _GATE_END), seg(_GATE_END, _Z_END),
        seg(_DT_END, _Q_END)[:, :, perm], seg(_Q_END, _K_END), seg(_K_END, _V_END),
        seg(_Z_END, _XBC_END), seg(_V_END, _IN_COLS), pad_lanes(seg(_XBC_END, _DT_END))],
        axis=-1).astype(BF16)
    assert w_in_k.shape[-1] == C_END
    row = lambda a: a.astype(F32)[:, None, :]
    att_bias = jnp.moveaxis(rel_bias.astype(F32)[_t5_bucket_table()], -1, 0)
    att_bias = att_bias.reshape(ATT_N_HEADS * ATT_CHUNK, ATT_BAND)
    return {
        "norm1_w": row(norm1_w), "w_in": w_in_k,
        "conv_w": ssd_conv_w.astype(F32), "conv_b": row(ssd_conv_b),
        "dt_bias": row(pad_lanes(ssd_dt_bias)), "a_log": row(pad_lanes(ssd_a_log)),
        "d_skip": row(jnp.repeat(ssd_d, SSD_HEAD_DIM, axis=-1)), "ssd_norm_w": row(ssd_norm_w),
        "sinks": attn_sinks.astype(F32), "att_bias": att_bias,
        "pool_w": pool_w.astype(BF16), "pool_scale": row(pool_scale),
        "w_up_ssd": w_up_ssd.astype(BF16), "w_up_attn": w_up_attn[:, perm, :].astype(BF16),
        "w_up_pool": w_up_pool.astype(BF16), "w_out": w_out.astype(BF16),
        "norm2_w": row(norm2_w), "w_mlp_in": w_mlp_in.astype(BF16), "w_mlp_out": w_mlp_out.astype(BF16),
        "final_norm_w": final_norm_w.astype(F32)[None, :], "depth": depth,
    }


def kernel(x, norm1_w, w_in, ssd_conv_w, ssd_conv_b, ssd_dt_bias, ssd_a_log, ssd_d, ssd_norm_w, attn_sinks, rel_bias, pool_w, pool_scale, w_up_ssd, w_up_attn, w_up_pool, w_out, norm2_w, w_mlp_in, w_mlp_out, final_norm_w):
    p = _prepare_params(norm1_w, w_in, ssd_conv_w, ssd_conv_b, ssd_dt_bias, ssd_a_log, ssd_d, ssd_norm_w,
                        attn_sinks, rel_bias, pool_w, pool_scale, w_up_ssd, w_up_attn, w_up_pool, w_out,
                        norm2_w, w_mlp_in, w_mlp_out, final_norm_w)
    b, s, d = x.shape
    seq_tile = min(SEQ_TILE, s // 2)
    mlp_tile = min(MLP_TILE, b * s)
    depth = p.pop("depth")
    x = x.astype(F32)
    xn = _norm_call(x.reshape(b * s, d), p, mlp_tile)
    for layer in range(depth):
        x = _mixer_call(x, xn, layer, p, seq_tile)
        if layer == depth - 1:
            return _mlp_call(x.reshape(b * s, d), layer, p, mlp_tile, True).reshape(b, s, d)
        x, xn = _mlp_call(x.reshape(b * s, d), layer, p, mlp_tile, False)
        x = x.reshape(b, s, d)
```

```python
import functools
import math

import numpy as np
import jax
import jax.numpy as jnp
from jax import lax
from jax.experimental import pallas as pl
from jax.experimental.pallas import tpu as pltpu

F32 = jnp.float32
BF16 = jnp.bfloat16

D_MODEL = 1024
EPS = 1e-6

SSD_HEAD_DIM = 64
SSD_N_HEADS = 16
SSD_N_GROUPS = 2
SSD_D_STATE = 128
SSD_CONV = 4
SSD_GROUP_WIDTH = D_MODEL // SSD_N_GROUPS
SSD_CONV_DIM = D_MODEL + 2 * SSD_N_GROUPS * SSD_D_STATE

ATT_HEAD_DIM = 64
ATT_N_HEADS = 8
ATT_N_KV = 2
ATT_CHUNK = 64
ATT_PAD = 128
ATT_BAND = ATT_PAD + ATT_CHUNK
N_BUCKETS = 32
MAX_DISTANCE = 128

POOL_WIDTH = 512
POOL_WINDOWS = (2, 4, 8, 16)
POOL_GROUP = 128
POOL_HIST = 16
CONV_HIST = 8

D_FF = 4 * D_MODEL

LANES = 128

_GATE_END = 3 * D_MODEL
_Z_END = _GATE_END + D_MODEL
_XBC_END = _Z_END + SSD_CONV_DIM
_DT_END = _XBC_END + SSD_N_HEADS
_Q_END = _DT_END + ATT_N_HEADS * ATT_HEAD_DIM
_K_END = _Q_END + ATT_N_KV * ATT_HEAD_DIM
_V_END = _K_END + ATT_N_KV * ATT_HEAD_DIM
_IN_COLS = _V_END + POOL_WIDTH

B_GATE = 0
B_Z = B_GATE + 3 * D_MODEL
B_Q = B_Z + D_MODEL
B_K = B_Q + ATT_N_HEADS * ATT_HEAD_DIM
B_V = B_K + LANES
PB_COLS = B_V + LANES
F_XBC = 0
F_U = F_XBC + SSD_CONV_DIM
F_DT = F_U + POOL_WIDTH
PF_COLS = F_DT + LANES
C_END = PB_COLS + PF_COLS
PROJ_SEGMENT = 512
PB_SEGMENTS = tuple((a, min(a + PROJ_SEGMENT, PB_COLS)) for a in range(0, PB_COLS, PROJ_SEGMENT))
PF_SEGMENTS = tuple((a, a + PROJ_SEGMENT) for a in range(0, F_U, PROJ_SEGMENT)) + ((F_U, PF_COLS),)

SEQ_TILE = 256
SSD_CHUNK = LANES
MLP_TILE = 512
VMEM_LIMIT = 60 * 1024 * 1024


def _att_head_perm():
    cols = np.arange(ATT_N_HEADS * ATT_HEAD_DIM)
    tile, half, d = cols // LANES, (cols % LANES) // ATT_HEAD_DIM, cols % ATT_HEAD_DIM
    return (tile + (ATT_N_HEADS // ATT_N_KV) * half) * ATT_HEAD_DIM + d


def _t5_bucket_table():
    nb = N_BUCKETS // 2
    qpos = jnp.arange(ATT_CHUNK, dtype=jnp.int32)
    kpos = jnp.arange(ATT_BAND, dtype=jnp.int32) - ATT_PAD
    rel = kpos[None, :] - qpos[:, None]
    out = (rel > 0).astype(jnp.int32) * nb
    n = jnp.abs(rel)
    max_exact = nb // 2
    nf = jnp.maximum(n, 1).astype(jnp.float32)
    large = max_exact + (jnp.log(nf / max_exact) / math.log(MAX_DISTANCE / max_exact)
                         * (nb - max_exact)).astype(jnp.int32)
    large = jnp.minimum(large, nb - 1)
    return out + jnp.where(n < max_exact, n, large)


def _rmsnorm(x, w):
    return x * lax.rsqrt(jnp.mean(x * x, axis=-1, keepdims=True) + EPS) * w


def _dot(a, b):
    return jnp.dot(a, b, preferred_element_type=F32)


def _dot_nt(a, b):
    return lax.dot_general(a, b, (((1,), (1,)), ((), ())), preferred_element_type=F32)


def _dot_tn(a, b):
    return lax.dot_general(a, b, (((0,), (0,)), ((), ())), preferred_element_type=F32)


def _mixer_kernel(x2_ref, h2_ref, hn_ref, win_ref, convw_ref, convb_ref, dtb_ref, alog_ref, dskip_ref,
                  ssdnw_ref, sink_ref, bias_ref, poolw_ref, pools_ref, wus_ref, wua_ref, wup_ref,
                  wout_ref, o_ref, pb0, pf0, pb1, pf1, xbc_buf, k_buf, v_buf, u_buf, state_ref,
                  *, tile, tiles_per_row):
    T = tile
    g = pl.program_id(0)
    tile_in_row = lax.rem(2 * g, tiles_per_row)

    def projection_steps(h_ref, rows, pb, pf):
        def park_b(a, b):
            pb[:, a:b] = _dot(h_ref[rows, :], win_ref[:, a:b]).astype(BF16)

        def park_f(a, b):
            pf[:, a:b] = _dot(h_ref[rows, :], win_ref[:, PB_COLS + a:PB_COLS + b])

        return ([functools.partial(park_b, a, b) for a, b in PB_SEGMENTS]
                + [functools.partial(park_f, a, b) for a, b in PF_SEGMENTS])

    first, second = slice(0, T), slice(T, 2 * T)

    @pl.when(g == 0)
    def _():
        for step in projection_steps(h2_ref, first, pb0, pf0):
            step()

    @pl.when(tile_in_row == 0)
    def _():
        xbc_buf[:, 0:CONV_HIST, :] = jnp.zeros((SSD_CONV_DIM // LANES, CONV_HIST, LANES), F32)
        k_buf[0:ATT_PAD, :] = jnp.zeros((ATT_PAD, LANES), BF16)
        v_buf[0:ATT_PAD, :] = jnp.zeros((ATT_PAD, LANES), BF16)
        u_buf[:, 0:POOL_HIST, :] = jnp.zeros((POOL_WIDTH // LANES, POOL_HIST, LANES), F32)
        state_ref[...] = jnp.zeros(state_ref.shape, F32)

    lane_lo = lax.broadcasted_iota(jnp.int32, (1, LANES), 1) < (LANES // 2)
    row_i = lax.broadcasted_iota(jnp.int32, (T, T), 0)
    col_i = lax.broadcasted_iota(jnp.int32, (T, T), 1)
    tril = jnp.where(row_i >= col_i, 1.0, 0.0).astype(BF16)
    causal = (lax.broadcasted_iota(jnp.int32, (SSD_CHUNK, SSD_CHUNK), 0)
              >= lax.broadcasted_iota(jnp.int32, (SSD_CHUNK, SSD_CHUNK), 1))

    ISSUE_POINTS = 18

    def mix(pb, pf, rows, t0, ahead):
        ahead = list(ahead)
        points_left = [ISSUE_POINTS]

        def issue():
            for _ in range(-(-len(ahead) // points_left[0])):
                ahead.pop(0)()
            points_left[0] -= 1

        issue()
        conv_blocks = []
        for cb in range(SSD_CONV_DIM // LANES):
            cs = slice(cb * LANES, (cb + 1) * LANES)
            xbc_buf[cb, CONV_HIST:CONV_HIST + T, :] = pf[:, F_XBC + cb * LANES:F_XBC + (cb + 1) * LANES]
            conv = convb_ref[:, cs]
            for jj in range(SSD_CONV):
                off = CONV_HIST - (SSD_CONV - 1) + jj
                conv = conv + convw_ref[jj:jj + 1, cs] * xbc_buf[cb, off:off + T, :]
            conv_blocks.append(conv * jax.nn.sigmoid(conv))
            xbc_buf[cb, 0:CONV_HIST, :] = xbc_buf[cb, T:T + CONV_HIST, :]
            if cb % 4 == 3:
                issue()
        xbc = jnp.concatenate(conv_blocks, axis=1)
        xs = xbc[:, :D_MODEL]
        bmat = xbc[:, D_MODEL:D_MODEL + SSD_N_GROUPS * SSD_D_STATE].astype(BF16)
        cmat = xbc[:, D_MODEL + SSD_N_GROUPS * SSD_D_STATE:].astype(BF16)

        dt = jax.nn.softplus(pf[:, F_DT:PF_COLS] + dtb_ref[...])
        da = dt * (-jnp.exp(alog_ref[...]))
        da_hi = da.astype(BF16)
        da_r1 = da - da_hi.astype(F32)
        da_mid = da_r1.astype(BF16)
        da_lo = (da_r1 - da_mid.astype(F32)).astype(BF16)
        acs3 = _dot(tril, jnp.concatenate([da_hi, da_mid, da_lo], axis=1))
        acs = acs3[:, :LANES] + acs3[:, LANES:2 * LANES] + acs3[:, 2 * LANES:]
        acs_t = acs.T

        def head_cols(arr):
            return [jnp.broadcast_to(arr[:, h:h + 1], (T, LANES)) for h in range(SSD_N_HEADS)]

        def per_channel(cols):
            return jnp.concatenate([jnp.where(lane_lo, cols[2 * p], cols[2 * p + 1])
                                    for p in range(SSD_N_HEADS // 2)], axis=1)

        acs_cols = head_cols(acs)
        acs_ch = per_channel(acs_cols)
        dt_ch = per_channel(head_cols(dt))
        xdt = xs * dt_ch
        xdt_b = xdt.astype(BF16)
        n_chunks = T // SSD_CHUNK
        chunk_rows = [slice(k * SSD_CHUNK, (k + 1) * SSD_CHUNK) for k in range(n_chunks)]
        ends = [acs_ch[(k + 1) * SSD_CHUNK - 1:(k + 1) * SSD_CHUNK, :] for k in range(n_chunks)]
        grow, xdt_tail_b, chunk_decay = [], [], []
        for k, rk in enumerate(chunk_rows):
            start = ends[k - 1] if k else None
            grow.append(jnp.exp(acs_ch[rk] - start if k else acs_ch[rk]))
            xdt_tail_b.append((xdt[rk] * jnp.exp(ends[k] - acs_ch[rk])).astype(BF16))
            chunk_decay.append(jnp.exp(ends[k] - start if k else ends[k]))
        issue()

        y_rows = [[None] * (D_MODEL // LANES) for _ in range(n_chunks)]
        for grp in range(SSD_N_GROUPS):
            gs = slice(grp * SSD_GROUP_WIDTH, (grp + 1) * SSD_GROUP_WIDTH)
            state = state_ref[grp]
            for k, rk in enumerate(chunk_rows):
                bm_k = bmat[rk, grp * SSD_D_STATE:(grp + 1) * SSD_D_STATE]
                cm_k = cmat[rk, grp * SSD_D_STATE:(grp + 1) * SSD_D_STATE]
                cb = _dot_nt(cm_k, bm_k)
                y_off = _dot(cm_k, state.astype(BF16)) * grow[k][:, gs]
                state = state * chunk_decay[k][:, gs] + _dot_tn(bm_k, xdt_tail_b[k][:, gs])
                for pp in range(SSD_GROUP_WIDTH // LANES):
                    p = grp * (SSD_GROUP_WIDTH // LANES) + pp
                    blk = xdt_b[rk, p * LANES:(p + 1) * LANES]
                    m_pair = []
                    for h in (2 * p, 2 * p + 1):
                        seg = acs_cols[h][rk] - acs_t[h:h + 1, rk]
                        lmat = jnp.exp(jnp.where(causal, seg, -jnp.inf))
                        m_pair.append((cb * lmat).astype(BF16))
                    zero = jnp.zeros_like(blk)
                    x_pair = jnp.concatenate([jnp.where(lane_lo, blk, zero), jnp.where(lane_lo, zero, blk)], axis=0)
                    y_rows[k][p] = (y_off[:, pp * LANES:(pp + 1) * LANES]
                                    + _dot(jnp.concatenate(m_pair, axis=1), x_pair))
                issue()
            state_ref[grp] = state
        y = jnp.concatenate([jnp.concatenate(r, axis=1) for r in y_rows], axis=0) + xs * dskip_ref[...]
        z = pb[:, B_Z:B_Q].astype(F32)
        y = y * (z * jax.nn.sigmoid(z))
        y_norm = []
        for grp in range(SSD_N_GROUPS):
            yg = y[:, grp * SSD_GROUP_WIDTH:(grp + 1) * SSD_GROUP_WIDTH]
            y_norm.append(yg * lax.rsqrt(jnp.mean(yg * yg, axis=-1, keepdims=True) + EPS))
        y_ssd = (jnp.concatenate(y_norm, axis=1) * ssdnw_ref[...]).astype(BF16)
        issue()

        q = pb[:, B_Q:B_K] * (ATT_HEAD_DIM ** -0.5)
        k_buf[ATT_PAD:ATT_PAD + T, :] = pb[:, B_K:B_V]
        v_buf[ATT_PAD:ATT_PAD + T, :] = pb[:, B_V:PB_COLS]
        key_i = lax.broadcasted_iota(jnp.int32, (1, ATT_BAND), 1)
        rep = ATT_N_HEADS // ATT_N_KV
        sink_rows = jnp.concatenate([jnp.full((ATT_CHUNK, LANES), sink_ref[hq], F32)
                                     for hq in range(ATT_N_HEADS)], axis=0)
        att_rows = []
        for c in range(T // ATT_CHUNK):
            r0 = c * ATT_CHUNK
            kband = k_buf[r0:r0 + ATT_BAND, :]
            vband = v_buf[r0:r0 + ATT_BAND, :]
            q_stack = []
            for hq in range(ATT_N_HEADS):
                qblk = q[r0:r0 + ATT_CHUNK, (hq % rep) * LANES:(hq % rep + 1) * LANES]
                keep = lane_lo if hq < rep else jnp.logical_not(lane_lo)
                q_stack.append(jnp.where(keep, qblk, jnp.zeros_like(qblk)))
            s = _dot_nt(jnp.concatenate(q_stack, axis=0), kband) + bias_ref[...]
            if r0 < ATT_PAD:
                s = jnp.where((key_i + (r0 - ATT_PAD) + t0) >= 0, s, -jnp.inf)
            m = jnp.maximum(jnp.max(s, axis=-1, keepdims=True), sink_rows)
            pr = jnp.exp(s - jnp.concatenate([m, m[:, :ATT_BAND - LANES]], axis=1))
            den = jnp.sum(pr, axis=-1, keepdims=True) + jnp.exp(sink_rows - m)
            o = _dot(pr.astype(BF16), vband) * (1.0 / den)
            att_rows.append(jnp.concatenate(
                [jnp.where(lane_lo, o[jb * ATT_CHUNK:(jb + 1) * ATT_CHUNK],
                           o[(jb + rep) * ATT_CHUNK:(jb + rep + 1) * ATT_CHUNK]) for jb in range(rep)], axis=1))
            issue()
        y_att = jnp.concatenate(att_rows, axis=0).astype(BF16)
        k_buf[0:ATT_PAD, :] = k_buf[T:T + ATT_PAD, :]
        v_buf[0:ATT_PAD, :] = v_buf[T:T + ATT_PAD, :]

        tpos = t0 + lax.broadcasted_iota(jnp.int32, (T, 1), 0)
        pool_tiles = []
        for gi, w in enumerate(POOL_WINDOWS):
            ug = pf[:, F_U + gi * POOL_GROUP:F_U + (gi + 1) * POOL_GROUP]
            u_buf[gi, POOL_HIST:POOL_HIST + T, :] = ug
            acc = ug
            for back in range(1, w):
                acc = acc + u_buf[gi, POOL_HIST - back:POOL_HIST - back + T, :]
            u_buf[gi, 0:POOL_HIST, :] = u_buf[gi, T:T + POOL_HIST, :]
            cnt = jnp.minimum(tpos + 1, w).astype(F32)
            pooled = acc / cnt - ug
            pool_tiles.append(_dot(pooled.astype(BF16), poolw_ref[gi]))
            issue()
        y_pool = (jnp.concatenate(pool_tiles, axis=1) * pools_ref[...]).astype(BF16)
        assert points_left[0] == 0 and not ahead

        acc = []

        def branch(i, y_branch, w_ref):
            gate = jax.nn.sigmoid(pb[:, B_GATE + i * D_MODEL:B_GATE + (i + 1) * D_MODEL].astype(F32))
            acc.append(gate * _dot(y_branch, w_ref[...]))

        def finish():
            merged = acc[0] + acc[1] + acc[2]
            o_ref[rows, :] = x2_ref[rows, :] + _dot(merged.astype(BF16), wout_ref[...])

        return [functools.partial(branch, 0, y_ssd, wus_ref), functools.partial(branch, 1, y_att, wua_ref),
                functools.partial(branch, 2, y_pool, wup_ref), finish]

    merge_first = mix(pb0, pf0, first, tile_in_row * T, projection_steps(h2_ref, second, pb1, pf1))
    proj_next = projection_steps(hn_ref, slice(None), pb0, pf0)
    n_gate = sum(1 for a, _ in PB_SEGMENTS if a < B_Z)
    per_gate = n_gate // 3
    park_gate, park_rest, park_f = proj_next[0:n_gate], proj_next[n_gate:len(PB_SEGMENTS)], proj_next[len(PB_SEGMENTS):]
    ahead = list(park_f)
    for i in range(3):
        ahead += merge_first[i:i + 1] + park_gate[i * per_gate:(i + 1) * per_gate]
    ahead += park_rest + merge_first[3:4]
    for step in mix(pb1, pf1, second, (tile_in_row + 1) * T, ahead):
        step()


def _mlp_kernel(x_ref, n2_ref, w1_ref, w2_ref, nnext_ref, o_ref, *maybe_hn_ref, last):
    x = x_ref[...]
    hb = _rmsnorm(x, n2_ref[...]).astype(BF16)
    a = jnp.maximum(_dot(hb, w1_ref[...]), 0.0)
    out = x + _dot((a * a).astype(BF16), w2_ref[...])
    if last:
        o_ref[...] = _rmsnorm(out, nnext_ref[...])
    else:
        o_ref[...] = out
        maybe_hn_ref[0][...] = _rmsnorm(out, nnext_ref[...]).astype(BF16)


def _norm_kernel(x_ref, w_ref, o_ref):
    o_ref[...] = _rmsnorm(x_ref[...], w_ref[...]).astype(BF16)


def _resident(shape, index):
    return pl.BlockSpec(shape, index, pipeline_mode=pl.Buffered(1))


def _mixer_call(x, xn, layer, p, tile):
    b, s, _ = x.shape
    tiles_per_row = s // tile
    assert s % tile == 0 and tiles_per_row % 2 == 0 and tile % LANES == 0 and tile >= ATT_PAD
    n_tiles = b * tiles_per_row
    l3 = lambda g: (layer, 0, 0)
    l4 = lambda g: (layer, 0, 0, 0)
    vec = lambda n: _resident((None, 1, n), l3)
    in_specs = [
        pl.BlockSpec((None, 2 * tile, D_MODEL), lambda g: (g, 0, 0)),
        pl.BlockSpec((None, 2 * tile, D_MODEL), lambda g: (g, 0, 0)),
        pl.BlockSpec((None, tile, D_MODEL), lambda g: (jnp.minimum(2 * g + 2, n_tiles - 1), 0, 0)),
        _resident((None, D_MODEL, C_END), l3),
        _resident((None, SSD_CONV, SSD_CONV_DIM), l3),
        vec(SSD_CONV_DIM),
        vec(LANES),
        vec(LANES),
        vec(D_MODEL),
        vec(D_MODEL),
        pl.BlockSpec(memory_space=pltpu.SMEM),
        _resident((ATT_N_HEADS * ATT_CHUNK, ATT_BAND), lambda g: (0, 0)),
        _resident((None, len(POOL_WINDOWS), POOL_GROUP, POOL_GROUP), l4),
        vec(POOL_WIDTH),
        _resident((None, D_MODEL, D_MODEL), l3),
        _resident((None, ATT_N_HEADS * ATT_HEAD_DIM, D_MODEL), l3),
        _resident((None, POOL_WIDTH, D_MODEL), l3),
        _resident((None, D_MODEL, D_MODEL), l3),
    ]
    out = pl.pallas_call(
        functools.partial(_mixer_kernel, tile=tile, tiles_per_row=tiles_per_row),
        grid=(n_tiles // 2,),
        in_specs=in_specs,
        out_specs=pl.BlockSpec((None, 2 * tile, D_MODEL), lambda g: (g, 0, 0)),
        out_shape=jax.ShapeDtypeStruct((n_tiles // 2, 2 * tile, D_MODEL), F32),
        scratch_shapes=[
            pltpu.VMEM((tile, PB_COLS), BF16), pltpu.VMEM((tile, PF_COLS), F32),
            pltpu.VMEM((tile, PB_COLS), BF16), pltpu.VMEM((tile, PF_COLS), F32),
            pltpu.VMEM((SSD_CONV_DIM // LANES, CONV_HIST + tile, LANES), F32),
            pltpu.VMEM((ATT_PAD + tile, LANES), BF16),
            pltpu.VMEM((ATT_PAD + tile, LANES), BF16),
            pltpu.VMEM((POOL_WIDTH // LANES, POOL_HIST + tile, LANES), F32),
            pltpu.VMEM((SSD_N_GROUPS, SSD_D_STATE, SSD_GROUP_WIDTH), F32),
        ],
        compiler_params=pltpu.CompilerParams(
            dimension_semantics=("arbitrary",), vmem_limit_bytes=VMEM_LIMIT),
        name="mixer",
    )(x.reshape(n_tiles // 2, 2 * tile, D_MODEL), xn.reshape(n_tiles // 2, 2 * tile, D_MODEL),
      xn.reshape(n_tiles, tile, D_MODEL), p["w_in"], p["conv_w"], p["conv_b"], p["dt_bias"], p["a_log"], p["d_skip"],
      p["ssd_norm_w"], p["sinks"][layer], p["att_bias"], p["pool_w"], p["pool_scale"],
      p["w_up_ssd"], p["w_up_attn"], p["w_up_pool"], p["w_out"])
    return out.reshape(b, s, D_MODEL)


def _mlp_call(x2d, layer, p, tile, last):
    n, _ = x2d.shape
    assert n % tile == 0
    rows = pl.BlockSpec((tile, D_MODEL), lambda i: (i, 0))
    if last:
        next_norm = _resident((1, D_MODEL), lambda i: (0, 0))
        next_w, out_specs, out_shape = p["final_norm_w"], rows, jax.ShapeDtypeStruct(x2d.shape, F32)
    else:
        next_norm = _resident((None, 1, D_MODEL), lambda i: (layer + 1, 0, 0))
        next_w, out_specs = p["norm1_w"], (rows, rows)
        out_shape = (jax.ShapeDtypeStruct(x2d.shape, F32), jax.ShapeDtypeStruct(x2d.shape, BF16))
    l3 = lambda i: (layer, 0, 0)
    return pl.pallas_call(
        functools.partial(_mlp_kernel, last=last),
        grid=(n // tile,),
        in_specs=[rows, _resident((None, 1, D_MODEL), l3), _resident((None, D_MODEL, D_FF), l3),
                  _resident((None, D_FF, D_MODEL), l3), next_norm],
        out_specs=out_specs,
        out_shape=out_shape,
        compiler_params=pltpu.CompilerParams(
            dimension_semantics=("arbitrary",), vmem_limit_bytes=VMEM_LIMIT),
        name="mlp",
    )(x2d, p["norm2_w"], p["w_mlp_in"], p["w_mlp_out"], next_w)


def _norm_call(x2d, p, tile):
    n, _ = x2d.shape
    rows = pl.BlockSpec((tile, D_MODEL), lambda i: (i, 0))
    return pl.pallas_call(
        _norm_kernel, grid=(n // tile,),
        in_specs=[rows, _resident((None, 1, D_MODEL), lambda i: (0, 0, 0))],
        out_specs=rows, out_shape=jax.ShapeDtypeStruct(x2d.shape, BF16), name="norm",
    )(x2d, p["norm1_w"])


def _prepare_params(norm1_w, w_in, ssd_conv_w, ssd_conv_b, ssd_dt_bias, ssd_a_log, ssd_d, ssd_norm_w,
                    attn_sinks, rel_bias, pool_w, pool_scale, w_up_ssd, w_up_attn, w_up_pool, w_out,
                    norm2_w, w_mlp_in, w_mlp_out, final_norm_w):
    depth = w_in.shape[0]
    perm = _att_head_perm()
    seg = lambda a, b: w_in[:, :, a:b]
    pad_lanes = lambda a: jnp.pad(a, ((0, 0),) * (a.ndim - 1) + ((0, LANES - a.shape[-1]),))
    w_in_k = jnp.concatenate([
        seg(0, _GATE_END), seg(_GATE_END, _Z_END),
        seg(_DT_END, _Q_END)[:, :, perm], seg(_Q_END, _K_END), seg(_K_END, _V_END),
        seg(_Z_END, _XBC_END), seg(_V_END, _IN_COLS), pad_lanes(seg(_XBC_END, _DT_END))],
        axis=-1).astype(BF16)
    assert w_in_k.shape[-1] == C_END
    row = lambda a: a.astype(F32)[:, None, :]
    att_bias = jnp.moveaxis(rel_bias.astype(F32)[_t5_bucket_table()], -1, 0)
    att_bias = att_bias.reshape(ATT_N_HEADS * ATT_CHUNK, ATT_BAND)
    return {
        "norm1_w": row(norm1_w), "w_in": w_in_k,
        "conv_w": ssd_conv_w.astype(F32), "conv_b": row(ssd_conv_b),
        "dt_bias": row(pad_lanes(ssd_dt_bias)), "a_log": row(pad_lanes(ssd_a_log)),
        "d_skip": row(jnp.repeat(ssd_d, SSD_HEAD_DIM, axis=-1)), "ssd_norm_w": row(ssd_norm_w),
        "sinks": attn_sinks.astype(F32), "att_bias": att_bias,
        "pool_w": pool_w.astype(BF16), "pool_scale": row(pool_scale),
        "w_up_ssd": w_up_ssd.astype(BF16), "w_up_attn": w_up_attn[:, perm, :].astype(BF16),
        "w_up_pool": w_up_pool.astype(BF16), "w_out": w_out.astype(BF16),
        "norm2_w": row(norm2_w), "w_mlp_in": w_mlp_in.astype(BF16), "w_mlp_out": w_mlp_out.astype(BF16),
        "final_norm_w": final_norm_w.astype(F32)[None, :], "depth": depth,
    }


def kernel(x, norm1_w, w_in, ssd_conv_w, ssd_conv_b, ssd_dt_bias, ssd_a_log, ssd_d, ssd_norm_w, attn_sinks, rel_bias, pool_w, pool_scale, w_up_ssd, w_up_attn, w_up_pool, w_out, norm2_w, w_mlp_in, w_mlp_out, final_norm_w):
    p = _prepare_params(norm1_w, w_in, ssd_conv_w, ssd_conv_b, ssd_dt_bias, ssd_a_log, ssd_d, ssd_norm_w,
                        attn_sinks, rel_bias, pool_w, pool_scale, w_up_ssd, w_up_attn, w_up_pool, w_out,
                        norm2_w, w_mlp_in, w_mlp_out, final_norm_w)
    b, s, d = x.shape
    seq_tile = min(SEQ_TILE, s // 2)
    mlp_tile = min(MLP_TILE, b * s)
    depth = p.pop("depth")
    x = x.astype(F32)
    xn = _norm_call(x.reshape(b * s, d), p, mlp_tile)
    for layer in range(depth):
        x = _mixer_call(x, xn, layer, p, seq_tile)
        if layer == depth - 1:
            return _mlp_call(x.reshape(b * s, d), layer, p, mlp_tile, True).reshape(b, s, d)
        x, xn = _mlp_call(x.reshape(b * s, d), layer, p, mlp_tile, False)
        x = x.reshape(b, s, d)
```

```python
import functools
import math

import jax
import jax.numpy as jnp
from jax import lax
from jax.experimental import pallas as pl
from jax.experimental.pallas import tpu as pltpu

F32 = jnp.float32
BF16 = jnp.bfloat16

D_MODEL = 1024
EPS = 1e-6

SSD_HEAD_DIM = 64
SSD_N_HEADS = 16
SSD_N_GROUPS = 2
SSD_D_STATE = 128
SSD_CONV = 4
SSD_GROUP_WIDTH = D_MODEL // SSD_N_GROUPS
SSD_CONV_DIM = D_MODEL + 2 * SSD_N_GROUPS * SSD_D_STATE

ATT_HEAD_DIM = 64
ATT_N_HEADS = 8
ATT_N_KV = 2
ATT_CHUNK = 64
ATT_PAD = 128
ATT_BAND = ATT_PAD + ATT_CHUNK
N_BUCKETS = 32
MAX_DISTANCE = 128

POOL_WIDTH = 512
POOL_WINDOWS = (2, 4, 8, 16)
POOL_GROUP = 128
POOL_HIST = 16
CONV_HIST = 8

D_FF = 4 * D_MODEL

LANES = 128

_GATE_END = 3 * D_MODEL
_Z_END = _GATE_END + D_MODEL
_XBC_END = _Z_END + SSD_CONV_DIM
_DT_END = _XBC_END + SSD_N_HEADS
_Q_END = _DT_END + ATT_N_HEADS * ATT_HEAD_DIM
_K_END = _Q_END + ATT_N_KV * ATT_HEAD_DIM
_V_END = _K_END + ATT_N_KV * ATT_HEAD_DIM
_IN_COLS = _V_END + POOL_WIDTH

B_GATE = 0
B_Z = B_GATE + 3 * D_MODEL
B_Q = B_Z + D_MODEL
B_K = B_Q + ATT_N_HEADS * ATT_HEAD_DIM
B_V = B_K + LANES
PB_COLS = B_V + LANES
F_XBC = 0
F_U = F_XBC + SSD_CONV_DIM
F_DT = F_U + POOL_WIDTH
PF_COLS = F_DT + LANES
C_END = PB_COLS + PF_COLS
PROJ_SEGMENT = 512
PB_SEGMENTS = tuple((a, min(a + PROJ_SEGMENT, PB_COLS)) for a in range(0, PB_COLS, PROJ_SEGMENT))
PF_SEGMENTS = tuple((a, a + PROJ_SEGMENT) for a in range(0, F_U, PROJ_SEGMENT)) + ((F_U, PF_COLS),)

SEQ_TILE = 256
SSD_CHUNK = LANES
MLP_TILE = 512
NORM_TILE = 1024
VMEM_LIMIT = 60 * 1024 * 1024


def _pair_heads(a, axis):
    rep = ATT_N_HEADS // ATT_N_KV
    split = a.reshape(a.shape[:axis] + (ATT_N_KV, rep, ATT_HEAD_DIM) + a.shape[axis + 1:])
    return jnp.swapaxes(split, axis, axis + 1).reshape(a.shape)


def _t5_bucket_table():
    nb = N_BUCKETS // 2
    qpos = jnp.arange(ATT_CHUNK, dtype=jnp.int32)
    kpos = jnp.arange(ATT_BAND, dtype=jnp.int32) - ATT_PAD
    rel = kpos[None, :] - qpos[:, None]
    out = (rel > 0).astype(jnp.int32) * nb
    n = jnp.abs(rel)
    max_exact = nb // 2
    nf = jnp.maximum(n, 1).astype(jnp.float32)
    large = max_exact + (jnp.log(nf / max_exact) / math.log(MAX_DISTANCE / max_exact)
                         * (nb - max_exact)).astype(jnp.int32)
    large = jnp.minimum(large, nb - 1)
    return out + jnp.where(n < max_exact, n, large)


def _rmsnorm(x, w):
    return x * lax.rsqrt(jnp.mean(x * x, axis=-1, keepdims=True) + EPS) * w


def _dot(a, b):
    return jnp.dot(a, b, preferred_element_type=F32)


def _dot_nt(a, b):
    return lax.dot_general(a, b, (((1,), (1,)), ((), ())), preferred_element_type=F32)


def _dot_tn(a, b):
    return lax.dot_general(a, b, (((0,), (0,)), ((), ())), preferred_element_type=F32)


def _mixer_kernel(x2_ref, h2_ref, hn_ref, win_ref, convw_ref, convb_ref, dtb_ref, alog_ref, dskip_ref,
                  ssdnw_ref, sink_ref, bias_ref, poolw_ref, pools_ref, wus_ref, wua_ref, wup_ref,
                  wout_ref, o_ref, pb0, pf0, pb1, pf1, xbc_buf, k_buf, v_buf, u_buf, state_ref,
                  *, tile, tiles_per_row):
    T = tile
    g = pl.program_id(0)
    tile_in_row = lax.rem(2 * g, tiles_per_row)

    def projection_steps(h_ref, rows, pb, pf):
        def park_b(a, b):
            pb[:, a:b] = _dot(h_ref[rows, :], win_ref[:, a:b]).astype(BF16)

        def park_f(a, b):
            pf[:, a:b] = _dot(h_ref[rows, :], win_ref[:, PB_COLS + a:PB_COLS + b])

        return ([functools.partial(park_b, a, b) for a, b in PB_SEGMENTS]
                + [functools.partial(park_f, a, b) for a, b in PF_SEGMENTS])

    first, second = slice(0, T), slice(T, 2 * T)

    @pl.when(g == 0)
    def _():
        for step in projection_steps(h2_ref, first, pb0, pf0):
            step()

    @pl.when(tile_in_row == 0)
    def _():
        xbc_buf[:, 0:CONV_HIST, :] = jnp.zeros((SSD_CONV_DIM // LANES, CONV_HIST, LANES), F32)
        k_buf[0:ATT_PAD, :] = jnp.zeros((ATT_PAD, LANES), BF16)
        v_buf[0:ATT_PAD, :] = jnp.zeros((ATT_PAD, LANES), BF16)
        u_buf[:, 0:POOL_HIST, :] = jnp.zeros((POOL_WIDTH // LANES, POOL_HIST, LANES), F32)
        state_ref[...] = jnp.zeros(state_ref.shape, F32)

    lane_lo = lax.broadcasted_iota(jnp.int32, (1, LANES), 1) < (LANES // 2)
    row_i = lax.broadcasted_iota(jnp.int32, (T, T), 0)
    col_i = lax.broadcasted_iota(jnp.int32, (T, T), 1)
    tril = jnp.where(row_i >= col_i, 1.0, 0.0).astype(BF16)
    causal = (lax.broadcasted_iota(jnp.int32, (SSD_CHUNK, SSD_CHUNK), 0)
              >= lax.broadcasted_iota(jnp.int32, (SSD_CHUNK, SSD_CHUNK), 1))

    ISSUE_POINTS = 18

    def mix(pb, pf, rows, t0, ahead):
        ahead = list(ahead)
        points_left = [ISSUE_POINTS]

        def issue():
            for _ in range(-(-len(ahead) // points_left[0])):
                ahead.pop(0)()
            points_left[0] -= 1

        issue()
        conv_blocks = []
        for cb in range(SSD_CONV_DIM // LANES):
            cs = slice(cb * LANES, (cb + 1) * LANES)
            xbc_buf[cb, CONV_HIST:CONV_HIST + T, :] = pf[:, F_XBC + cb * LANES:F_XBC + (cb + 1) * LANES]
            conv = convb_ref[:, cs]
            for jj in range(SSD_CONV):
                off = CONV_HIST - (SSD_CONV - 1) + jj
                conv = conv + convw_ref[jj:jj + 1, cs] * xbc_buf[cb, off:off + T, :]
            conv_blocks.append(conv * jax.nn.sigmoid(conv))
            xbc_buf[cb, 0:CONV_HIST, :] = xbc_buf[cb, T:T + CONV_HIST, :]
            if cb % 4 == 3:
                issue()
        xbc = jnp.concatenate(conv_blocks, axis=1)
        xs = xbc[:, :D_MODEL]
        bmat = xbc[:, D_MODEL:D_MODEL + SSD_N_GROUPS * SSD_D_STATE].astype(BF16)
        cmat = xbc[:, D_MODEL + SSD_N_GROUPS * SSD_D_STATE:].astype(BF16)

        dt = jax.nn.softplus(pf[:, F_DT:PF_COLS] + dtb_ref[...])
        da = dt * (-jnp.exp(alog_ref[...]))
        da_hi = da.astype(BF16)
        da_r1 = da - da_hi.astype(F32)
        da_mid = da_r1.astype(BF16)
        da_lo = (da_r1 - da_mid.astype(F32)).astype(BF16)
        acs3 = _dot(tril, jnp.concatenate([da_hi, da_mid, da_lo], axis=1))
        acs = acs3[:, :LANES] + acs3[:, LANES:2 * LANES] + acs3[:, 2 * LANES:]
        acs_t = acs.T

        def head_cols(arr):
            return [jnp.broadcast_to(arr[:, h:h + 1], (T, LANES)) for h in range(SSD_N_HEADS)]

        def per_channel(cols):
            return jnp.concatenate([jnp.where(lane_lo, cols[2 * p], cols[2 * p + 1])
                                    for p in range(SSD_N_HEADS // 2)], axis=1)

        acs_cols = head_cols(acs)
        acs_ch = per_channel(acs_cols)
        dt_ch = per_channel(head_cols(dt))
        xdt = xs * dt_ch
        xdt_b = xdt.astype(BF16)
        n_chunks = T // SSD_CHUNK
        chunk_rows = [slice(k * SSD_CHUNK, (k + 1) * SSD_CHUNK) for k in range(n_chunks)]
        ends = [acs_ch[(k + 1) * SSD_CHUNK - 1:(k + 1) * SSD_CHUNK, :] for k in range(n_chunks)]
        grow, xdt_tail_b, chunk_decay = [], [], []
        for k, rk in enumerate(chunk_rows):
            start = ends[k - 1] if k else None
            grow.append(jnp.exp(acs_ch[rk] - start if k else acs_ch[rk]))
            xdt_tail_b.append((xdt[rk] * jnp.exp(ends[k] - acs_ch[rk])).astype(BF16))
            chunk_decay.append(jnp.exp(ends[k] - start if k else ends[k]))
        issue()

        y_rows = [[None] * (D_MODEL // LANES) for _ in range(n_chunks)]
        for grp in range(SSD_N_GROUPS):
            gs = slice(grp * SSD_GROUP_WIDTH, (grp + 1) * SSD_GROUP_WIDTH)
            state = state_ref[grp]
            for k, rk in enumerate(chunk_rows):
                bm_k = bmat[rk, grp * SSD_D_STATE:(grp + 1) * SSD_D_STATE]
                cm_k = cmat[rk, grp * SSD_D_STATE:(grp + 1) * SSD_D_STATE]
                cb = _dot_nt(cm_k, bm_k)
                y_off = _dot(cm_k, state.astype(BF16)) * grow[k][:, gs]
                state = state * chunk_decay[k][:, gs] + _dot_tn(bm_k, xdt_tail_b[k][:, gs])
                for pp in range(SSD_GROUP_WIDTH // LANES):
                    p = grp * (SSD_GROUP_WIDTH // LANES) + pp
                    blk = xdt_b[rk, p * LANES:(p + 1) * LANES]
                    m_pair = []
                    for h in (2 * p, 2 * p + 1):
                        seg = acs_cols[h][rk] - acs_t[h:h + 1, rk]
                        lmat = jnp.exp(jnp.where(causal, seg, -jnp.inf))
                        m_pair.append((cb * lmat).astype(BF16))
                    zero = jnp.zeros_like(blk)
                    x_pair = jnp.concatenate([jnp.where(lane_lo, blk, zero), jnp.where(lane_lo, zero, blk)], axis=0)
                    y_rows[k][p] = (y_off[:, pp * LANES:(pp + 1) * LANES]
                                    + _dot(jnp.concatenate(m_pair, axis=1), x_pair))
                issue()
            state_ref[grp] = state
        y = jnp.concatenate([jnp.concatenate(r, axis=1) for r in y_rows], axis=0) + xs * dskip_ref[...]
        z = pb[:, B_Z:B_Q].astype(F32)
        y = y * (z * jax.nn.sigmoid(z))
        y_norm = []
        for grp in range(SSD_N_GROUPS):
            yg = y[:, grp * SSD_GROUP_WIDTH:(grp + 1) * SSD_GROUP_WIDTH]
            y_norm.append(yg * lax.rsqrt(jnp.mean(yg * yg, axis=-1, keepdims=True) + EPS))
        y_ssd = (jnp.concatenate(y_norm, axis=1) * ssdnw_ref[...]).astype(BF16)
        issue()

        q = pb[:, B_Q:B_K] * (ATT_HEAD_DIM ** -0.5)
        k_buf[ATT_PAD:ATT_PAD + T, :] = pb[:, B_K:B_V]
        v_buf[ATT_PAD:ATT_PAD + T, :] = pb[:, B_V:PB_COLS]
        key_i = lax.broadcasted_iota(jnp.int32, (1, ATT_BAND), 1)
        rep = ATT_N_HEADS // ATT_N_KV
        sink_rows = jnp.concatenate([jnp.full((ATT_CHUNK, LANES), sink_ref[hq], F32)
                                     for hq in range(ATT_N_HEADS)], axis=0)
        att_rows = []
        for c in range(T // ATT_CHUNK):
            r0 = c * ATT_CHUNK
            kband = k_buf[r0:r0 + ATT_BAND, :]
            vband = v_buf[r0:r0 + ATT_BAND, :]
            q_stack = []
            for hq in range(ATT_N_HEADS):
                qblk = q[r0:r0 + ATT_CHUNK, (hq % rep) * LANES:(hq % rep + 1) * LANES]
                keep = lane_lo if hq < rep else jnp.logical_not(lane_lo)
                q_stack.append(jnp.where(keep, qblk, jnp.zeros_like(qblk)))
            s = _dot_nt(jnp.concatenate(q_stack, axis=0), kband) + bias_ref[...]
            if r0 < ATT_PAD:
                s = jnp.where((key_i + (r0 - ATT_PAD) + t0) >= 0, s, -jnp.inf)
            m = jnp.maximum(jnp.max(s, axis=-1, keepdims=True), sink_rows)
            pr = jnp.exp(s - jnp.concatenate([m, m[:, :ATT_BAND - LANES]], axis=1))
            den = jnp.sum(pr, axis=-1, keepdims=True) + jnp.exp(sink_rows - m)
            o = _dot(pr.astype(BF16), vband) * (1.0 / den)
            att_rows.append(jnp.concatenate(
                [jnp.where(lane_lo, o[jb * ATT_CHUNK:(jb + 1) * ATT_CHUNK],
                           o[(jb + rep) * ATT_CHUNK:(jb + rep + 1) * ATT_CHUNK]) for jb in range(rep)], axis=1))
            issue()
        y_att = jnp.concatenate(att_rows, axis=0).astype(BF16)
        k_buf[0:ATT_PAD, :] = k_buf[T:T + ATT_PAD, :]
        v_buf[0:ATT_PAD, :] = v_buf[T:T + ATT_PAD, :]

        tpos = t0 + lax.broadcasted_iota(jnp.int32, (T, 1), 0)
        pool_tiles = []
        for gi, w in enumerate(POOL_WINDOWS):
            ug = pf[:, F_U + gi * POOL_GROUP:F_U + (gi + 1) * POOL_GROUP]
            u_buf[gi, POOL_HIST:POOL_HIST + T, :] = ug
            acc = ug
            for back in range(1, w):
                acc = acc + u_buf[gi, POOL_HIST - back:POOL_HIST - back + T, :]
            u_buf[gi, 0:POOL_HIST, :] = u_buf[gi, T:T + POOL_HIST, :]
            cnt = jnp.minimum(tpos + 1, w).astype(F32)
            pooled = acc / cnt - ug
            pool_tiles.append(_dot(pooled.astype(BF16), poolw_ref[gi]))
            issue()
        y_pool = (jnp.concatenate(pool_tiles, axis=1) * pools_ref[...]).astype(BF16)
        assert points_left[0] == 0 and not ahead

        acc = []

        def branch(i, y_branch, w_ref):
            gate = jax.nn.sigmoid(pb[:, B_GATE + i * D_MODEL:B_GATE + (i + 1) * D_MODEL].astype(F32))
            acc.append(gate * _dot(y_branch, w_ref[...]))

        def finish():
            merged = acc[0] + acc[1] + acc[2]
            o_ref[rows, :] = x2_ref[rows, :] + _dot(merged.astype(BF16), wout_ref[...])

        return [functools.partial(branch, 0, y_ssd, wus_ref), functools.partial(branch, 1, y_att, wua_ref),
                functools.partial(branch, 2, y_pool, wup_ref), finish]

    merge_first = mix(pb0, pf0, first, tile_in_row * T, projection_steps(h2_ref, second, pb1, pf1))
    proj_next = projection_steps(hn_ref, slice(None), pb0, pf0)
    n_gate = sum(1 for a, _ in PB_SEGMENTS if a < B_Z)
    per_gate = n_gate // 3
    park_gate, park_rest, park_f = proj_next[0:n_gate], proj_next[n_gate:len(PB_SEGMENTS)], proj_next[len(PB_SEGMENTS):]
    ahead = list(park_f)
    for i in range(3):
        ahead += merge_first[i:i + 1] + park_gate[i * per_gate:(i + 1) * per_gate]
    ahead += park_rest + merge_first[3:4]
    for step in mix(pb1, pf1, second, (tile_in_row + 1) * T, ahead):
        step()


def _mlp_kernel(x_ref, n2_ref, w1_ref, w2_ref, nnext_ref, o_ref, *maybe_hn_ref, last):
    x = x_ref[...]
    hb = _rmsnorm(x, n2_ref[...]).astype(BF16)
    a = jnp.maximum(_dot(hb, w1_ref[...]), 0.0)
    out = x + _dot((a * a).astype(BF16), w2_ref[...])
    if last:
        o_ref[...] = _rmsnorm(out, nnext_ref[...])
    else:
        o_ref[...] = out
        maybe_hn_ref[0][...] = _rmsnorm(out, nnext_ref[...]).astype(BF16)


def _norm_kernel(x_ref, w_ref, o_ref):
    o_ref[...] = _rmsnorm(x_ref[...], w_ref[...]).astype(BF16)


def _resident(shape, index):
    return pl.BlockSpec(shape, index, pipeline_mode=pl.Buffered(1))


def _mixer_call(x, xn, layer, p, tile):
    b, s, _ = x.shape
    tiles_per_row = s // tile
    assert s % tile == 0 and tiles_per_row % 2 == 0 and tile % LANES == 0 and tile >= ATT_PAD
    n_tiles = b * tiles_per_row
    l3 = lambda g: (layer, 0, 0)
    l4 = lambda g: (layer, 0, 0, 0)
    vec = lambda n: _resident((None, 1, n), l3)
    in_specs = [
        pl.BlockSpec((None, 2 * tile, D_MODEL), lambda g: (g, 0, 0)),
        pl.BlockSpec((None, 2 * tile, D_MODEL), lambda g: (g, 0, 0)),
        pl.BlockSpec((None, tile, D_MODEL), lambda g: (jnp.minimum(2 * g + 2, n_tiles - 1), 0, 0)),
        _resident((None, D_MODEL, C_END), l3),
        _resident((None, SSD_CONV, SSD_CONV_DIM), l3),
        vec(SSD_CONV_DIM),
        vec(LANES),
        vec(LANES),
        vec(D_MODEL),
        vec(D_MODEL),
        pl.BlockSpec(memory_space=pltpu.SMEM),
        _resident((ATT_N_HEADS * ATT_CHUNK, ATT_BAND), lambda g: (0, 0)),
        _resident((None, len(POOL_WINDOWS), POOL_GROUP, POOL_GROUP), l4),
        vec(POOL_WIDTH),
        _resident((None, D_MODEL, D_MODEL), l3),
        _resident((None, ATT_N_HEADS * ATT_HEAD_DIM, D_MODEL), l3),
        _resident((None, POOL_WIDTH, D_MODEL), l3),
        _resident((None, D_MODEL, D_MODEL), l3),
    ]
    out = pl.pallas_call(
        functools.partial(_mixer_kernel, tile=tile, tiles_per_row=tiles_per_row),
        grid=(n_tiles // 2,),
        in_specs=in_specs,
        out_specs=pl.BlockSpec((None, 2 * tile, D_MODEL), lambda g: (g, 0, 0)),
        out_shape=jax.ShapeDtypeStruct((n_tiles // 2, 2 * tile, D_MODEL), F32),
        scratch_shapes=[
            pltpu.VMEM((tile, PB_COLS), BF16), pltpu.VMEM((tile, PF_COLS), F32),
            pltpu.VMEM((tile, PB_COLS), BF16), pltpu.VMEM((tile, PF_COLS), F32),
            pltpu.VMEM((SSD_CONV_DIM // LANES, CONV_HIST + tile, LANES), F32),
            pltpu.VMEM((ATT_PAD + tile, LANES), BF16),
            pltpu.VMEM((ATT_PAD + tile, LANES), BF16),
            pltpu.VMEM((POOL_WIDTH // LANES, POOL_HIST + tile, LANES), F32),
            pltpu.VMEM((SSD_N_GROUPS, SSD_D_STATE, SSD_GROUP_WIDTH), F32),
        ],
        compiler_params=pltpu.CompilerParams(
            dimension_semantics=("arbitrary",), vmem_limit_bytes=VMEM_LIMIT),
        name="mixer",
    )(x.reshape(n_tiles // 2, 2 * tile, D_MODEL), xn.reshape(n_tiles // 2, 2 * tile, D_MODEL),
      xn.reshape(n_tiles, tile, D_MODEL), p["w_in"], p["conv_w"], p["conv_b"], p["dt_bias"], p["a_log"], p["d_skip"],
      p["ssd_norm_w"], p["sinks"][layer], p["att_bias"], p["pool_w"], p["pool_scale"],
      p["w_up_ssd"], p["w_up_attn"], p["w_up_pool"], p["w_out"])
    return out.reshape(b, s, D_MODEL)


def _mlp_call(x2d, layer, p, tile, last):
    n, _ = x2d.shape
    assert n % tile == 0
    rows = pl.BlockSpec((tile, D_MODEL), lambda i: (i, 0))
    if last:
        next_norm = _resident((1, D_MODEL), lambda i: (0, 0))
        next_w, out_specs, out_shape = p["final_norm_w"], rows, jax.ShapeDtypeStruct(x2d.shape, F32)
    else:
        next_norm = _resident((None, 1, D_MODEL), lambda i: (layer + 1, 0, 0))
        next_w, out_specs = p["norm1_w"], (rows, rows)
        out_shape = (jax.ShapeDtypeStruct(x2d.shape, F32), jax.ShapeDtypeStruct(x2d.shape, BF16))
    l3 = lambda i: (layer, 0, 0)
    return pl.pallas_call(
        functools.partial(_mlp_kernel, last=last),
        grid=(n // tile,),
        in_specs=[rows, _resident((None, 1, D_MODEL), l3), _resident((None, D_MODEL, D_FF), l3),
                  _resident((None, D_FF, D_MODEL), l3), next_norm],
        out_specs=out_specs,
        out_shape=out_shape,
        compiler_params=pltpu.CompilerParams(
            dimension_semantics=("arbitrary",), vmem_limit_bytes=VMEM_LIMIT),
        name="mlp",
    )(x2d, p["norm2_w"], p["w_mlp_in"], p["w_mlp_out"], next_w)


def _norm_call(x2d, p, tile):
    n, _ = x2d.shape
    rows = pl.BlockSpec((tile, D_MODEL), lambda i: (i, 0))
    return pl.pallas_call(
        _norm_kernel, grid=(n // tile,),
        in_specs=[rows, _resident((None, 1, D_MODEL), lambda i: (0, 0, 0))],
        out_specs=rows, out_shape=jax.ShapeDtypeStruct(x2d.shape, BF16), name="norm",
    )(x2d, p["norm1_w"])


def _prepare_params(norm1_w, w_in, ssd_conv_w, ssd_conv_b, ssd_dt_bias, ssd_a_log, ssd_d, ssd_norm_w,
                    attn_sinks, rel_bias, pool_w, pool_scale, w_up_ssd, w_up_attn, w_up_pool, w_out,
                    norm2_w, w_mlp_in, w_mlp_out, final_norm_w):
    depth = w_in.shape[0]
    w_in_b = w_in.astype(BF16)
    seg = lambda a, b: w_in_b[:, :, a:b]
    pad_lanes = lambda a: jnp.pad(a, ((0, 0),) * (a.ndim - 1) + ((0, LANES - a.shape[-1]),))
    w_in_k = jnp.concatenate([
        seg(0, _GATE_END), seg(_GATE_END, _Z_END),
        _pair_heads(seg(_DT_END, _Q_END), 2), seg(_Q_END, _K_END), seg(_K_END, _V_END),
        seg(_Z_END, _XBC_END), seg(_V_END, _IN_COLS), pad_lanes(seg(_XBC_END, _DT_END))],
        axis=-1)
    assert w_in_k.shape[-1] == C_END
    row = lambda a: a.astype(F32)[:, None, :]
    att_bias = jnp.moveaxis(rel_bias.astype(F32)[_t5_bucket_table()], -1, 0)
    att_bias = att_bias.reshape(ATT_N_HEADS * ATT_CHUNK, ATT_BAND)
    return {
        "norm1_w": row(norm1_w), "w_in": w_in_k,
        "conv_w": ssd_conv_w.astype(F32), "conv_b": row(ssd_conv_b),
        "dt_bias": row(pad_lanes(ssd_dt_bias)), "a_log": row(pad_lanes(ssd_a_log)),
        "d_skip": row(jnp.repeat(ssd_d, SSD_HEAD_DIM, axis=-1)), "ssd_norm_w": row(ssd_norm_w),
        "sinks": attn_sinks.astype(F32), "att_bias": att_bias,
        "pool_w": pool_w.astype(BF16), "pool_scale": row(pool_scale),
        "w_up_ssd": w_up_ssd.astype(BF16), "w_up_attn": _pair_heads(w_up_attn, 1).astype(BF16),
        "w_up_pool": w_up_pool.astype(BF16), "w_out": w_out.astype(BF16),
        "norm2_w": row(norm2_w), "w_mlp_in": w_mlp_in.astype(BF16), "w_mlp_out": w_mlp_out.astype(BF16),
        "final_norm_w": final_norm_w.astype(F32)[None, :], "depth": depth,
    }


def kernel(x, norm1_w, w_in, ssd_conv_w, ssd_conv_b, ssd_dt_bias, ssd_a_log, ssd_d, ssd_norm_w, attn_sinks, rel_bias, pool_w, pool_scale, w_up_ssd, w_up_attn, w_up_pool, w_out, norm2_w, w_mlp_in, w_mlp_out, final_norm_w):
    p = _prepare_params(norm1_w, w_in, ssd_conv_w, ssd_conv_b, ssd_dt_bias, ssd_a_log, ssd_d, ssd_norm_w,
                        attn_sinks, rel_bias, pool_w, pool_scale, w_up_ssd, w_up_attn, w_up_pool, w_out,
                        norm2_w, w_mlp_in, w_mlp_out, final_norm_w)
    b, s, d = x.shape
    seq_tile = min(SEQ_TILE, s // 2)
    mlp_tile = min(MLP_TILE, b * s)
    depth = p.pop("depth")
    x = x.astype(F32)
    xn = _norm_call(x.reshape(b * s, d), p, min(NORM_TILE, b * s))
    for layer in range(depth):
        x = _mixer_call(x, xn, layer, p, seq_tile)
        if layer == depth - 1:
            return _mlp_call(x.reshape(b * s, d), layer, p, mlp_tile, True).reshape(b, s, d)
        x, xn = _mlp_call(x.reshape(b * s, d), layer, p, mlp_tile, False)
        x = x.reshape(b, s, d)
```

```python
import functools
import math

import jax
import jax.numpy as jnp
from jax import lax
from jax.experimental import pallas as pl
from jax.experimental.pallas import tpu as pltpu

F32 = jnp.float32
BF16 = jnp.bfloat16

D_MODEL = 1024
EPS = 1e-6

SSD_HEAD_DIM = 64
SSD_N_HEADS = 16
SSD_N_GROUPS = 2
SSD_D_STATE = 128
SSD_CONV = 4
SSD_GROUP_WIDTH = D_MODEL // SSD_N_GROUPS
SSD_CONV_DIM = D_MODEL + 2 * SSD_N_GROUPS * SSD_D_STATE

ATT_HEAD_DIM = 64
ATT_N_HEADS = 8
ATT_N_KV = 2
ATT_CHUNK = 64
ATT_PAD = 128
ATT_BAND = ATT_PAD + ATT_CHUNK
N_BUCKETS = 32
MAX_DISTANCE = 128

POOL_WIDTH = 512
POOL_WINDOWS = (2, 4, 8, 16)
POOL_GROUP = 128
POOL_HIST = 16
CONV_HIST = 8

D_FF = 4 * D_MODEL

LANES = 128

_GATE_END = 3 * D_MODEL
_Z_END = _GATE_END + D_MODEL
_XBC_END = _Z_END + SSD_CONV_DIM
_DT_END = _XBC_END + SSD_N_HEADS
_Q_END = _DT_END + ATT_N_HEADS * ATT_HEAD_DIM
_K_END = _Q_END + ATT_N_KV * ATT_HEAD_DIM
_V_END = _K_END + ATT_N_KV * ATT_HEAD_DIM
_IN_COLS = _V_END + POOL_WIDTH

B_GATE = 0
B_Z = B_GATE + 3 * D_MODEL
B_Q = B_Z + D_MODEL
B_K = B_Q + ATT_N_HEADS * ATT_HEAD_DIM
B_V = B_K + LANES
PB_COLS = B_V + LANES
F_XBC = 0
F_U = F_XBC + SSD_CONV_DIM
F_DT = F_U + POOL_WIDTH
PF_COLS = F_DT + LANES
C_END = PB_COLS + PF_COLS
PROJ_SEGMENT = 512
PB_SEGMENTS = tuple((a, min(a + PROJ_SEGMENT, PB_COLS)) for a in range(0, PB_COLS, PROJ_SEGMENT))
PF_SEGMENTS = tuple((a, a + PROJ_SEGMENT) for a in range(0, F_U, PROJ_SEGMENT)) + ((F_U, PF_COLS),)

SEQ_TILE = 256
SSD_CHUNK = LANES
MLP_TILE = 512
NORM_TILE = 1024
VMEM_LIMIT = 60 * 1024 * 1024


def _pair_heads(a, axis):
    rep = ATT_N_HEADS // ATT_N_KV
    split = a.reshape(a.shape[:axis] + (ATT_N_KV, rep, ATT_HEAD_DIM) + a.shape[axis + 1:])
    return jnp.swapaxes(split, axis, axis + 1).reshape(a.shape)


def _t5_bucket_table():
    nb = N_BUCKETS // 2
    qpos = jnp.arange(ATT_CHUNK, dtype=jnp.int32)
    kpos = jnp.arange(ATT_BAND, dtype=jnp.int32) - ATT_PAD
    rel = kpos[None, :] - qpos[:, None]
    out = (rel > 0).astype(jnp.int32) * nb
    n = jnp.abs(rel)
    max_exact = nb // 2
    nf = jnp.maximum(n, 1).astype(jnp.float32)
    large = max_exact + (jnp.log(nf / max_exact) / math.log(MAX_DISTANCE / max_exact)
                         * (nb - max_exact)).astype(jnp.int32)
    large = jnp.minimum(large, nb - 1)
    return out + jnp.where(n < max_exact, n, large)


def _rmsnorm(x, w):
    return x * lax.rsqrt(jnp.mean(x * x, axis=-1, keepdims=True) + EPS) * w


def _dot(a, b):
    return jnp.dot(a, b, preferred_element_type=F32)


def _dot_nt(a, b):
    return lax.dot_general(a, b, (((1,), (1,)), ((), ())), preferred_element_type=F32)


def _dot_tn(a, b):
    return lax.dot_general(a, b, (((0,), (0,)), ((), ())), preferred_element_type=F32)


def _mixer_kernel(h2_ref, hn_ref, win_ref, convw_ref, convb_ref, dtb_ref, alog_ref, dskip_ref,
                  ssdnw_ref, sink_ref, bias_ref, poolw_ref, pools_ref, wus_ref, wua_ref, wup_ref,
                  wout_ref, o_ref, pb0, pf0, pb1, pf1, xbc_buf, k_buf, v_buf, u_buf, state_ref,
                  *, tile, tiles_per_row):
    T = tile
    g = pl.program_id(0)
    tile_in_row = lax.rem(2 * g, tiles_per_row)

    def projection_steps(h_ref, rows, pb, pf):
        def park_b(a, b):
            pb[:, a:b] = _dot(h_ref[rows, :], win_ref[:, a:b]).astype(BF16)

        def park_f(a, b):
            pf[:, a:b] = _dot(h_ref[rows, :], win_ref[:, PB_COLS + a:PB_COLS + b])

        return ([functools.partial(park_b, a, b) for a, b in PB_SEGMENTS]
                + [functools.partial(park_f, a, b) for a, b in PF_SEGMENTS])

    first, second = slice(0, T), slice(T, 2 * T)

    @pl.when(g == 0)
    def _():
        for step in projection_steps(h2_ref, first, pb0, pf0):
            step()

    @pl.when(tile_in_row == 0)
    def _():
        xbc_buf[:, 0:CONV_HIST, :] = jnp.zeros((SSD_CONV_DIM // LANES, CONV_HIST, LANES), F32)
        k_buf[0:ATT_PAD, :] = jnp.zeros((ATT_PAD, LANES), BF16)
        v_buf[0:ATT_PAD, :] = jnp.zeros((ATT_PAD, LANES), BF16)
        u_buf[:, 0:POOL_HIST, :] = jnp.zeros((POOL_WIDTH // LANES, POOL_HIST, LANES), F32)
        state_ref[...] = jnp.zeros(state_ref.shape, F32)

    lane_lo = lax.broadcasted_iota(jnp.int32, (1, LANES), 1) < (LANES // 2)
    row_i = lax.broadcasted_iota(jnp.int32, (T, T), 0)
    col_i = lax.broadcasted_iota(jnp.int32, (T, T), 1)
    tril = jnp.where(row_i >= col_i, 1.0, 0.0).astype(BF16)
    causal = (lax.broadcasted_iota(jnp.int32, (SSD_CHUNK, SSD_CHUNK), 0)
              >= lax.broadcasted_iota(jnp.int32, (SSD_CHUNK, SSD_CHUNK), 1))

    ISSUE_POINTS = 18

    def mix(pb, pf, rows, t0, ahead):
        ahead = list(ahead)
        points_left = [ISSUE_POINTS]

        def issue():
            for _ in range(-(-len(ahead) // points_left[0])):
                ahead.pop(0)()
            points_left[0] -= 1

        issue()
        conv_blocks = []
        for cb in range(SSD_CONV_DIM // LANES):
            cs = slice(cb * LANES, (cb + 1) * LANES)
            xbc_buf[cb, CONV_HIST:CONV_HIST + T, :] = pf[:, F_XBC + cb * LANES:F_XBC + (cb + 1) * LANES]
            conv = convb_ref[:, cs]
            for jj in range(SSD_CONV):
                off = CONV_HIST - (SSD_CONV - 1) + jj
                conv = conv + convw_ref[jj:jj + 1, cs] * xbc_buf[cb, off:off + T, :]
            conv_blocks.append(conv * jax.nn.sigmoid(conv))
            xbc_buf[cb, 0:CONV_HIST, :] = xbc_buf[cb, T:T + CONV_HIST, :]
            if cb % 4 == 3:
                issue()
        xbc = jnp.concatenate(conv_blocks, axis=1)
        xs = xbc[:, :D_MODEL]
        bmat = xbc[:, D_MODEL:D_MODEL + SSD_N_GROUPS * SSD_D_STATE].astype(BF16)
        cmat = xbc[:, D_MODEL + SSD_N_GROUPS * SSD_D_STATE:].astype(BF16)

        dt = jax.nn.softplus(pf[:, F_DT:PF_COLS] + dtb_ref[...])
        da = dt * (-jnp.exp(alog_ref[...]))
        da_hi = da.astype(BF16)
        da_r1 = da - da_hi.astype(F32)
        da_mid = da_r1.astype(BF16)
        da_lo = (da_r1 - da_mid.astype(F32)).astype(BF16)
        acs3 = _dot(tril, jnp.concatenate([da_hi, da_mid, da_lo], axis=1))
        acs = acs3[:, :LANES] + acs3[:, LANES:2 * LANES] + acs3[:, 2 * LANES:]
        acs_t = acs.T

        def head_cols(arr):
            return [jnp.broadcast_to(arr[:, h:h + 1], (T, LANES)) for h in range(SSD_N_HEADS)]

        def per_channel(cols):
            return jnp.concatenate([jnp.where(lane_lo, cols[2 * p], cols[2 * p + 1])
                                    for p in range(SSD_N_HEADS // 2)], axis=1)

        acs_cols = head_cols(acs)
        acs_ch = per_channel(acs_cols)
        dt_ch = per_channel(head_cols(dt))
        xdt = xs * dt_ch
        xdt_b = xdt.astype(BF16)
        n_chunks = T // SSD_CHUNK
        chunk_rows = [slice(k * SSD_CHUNK, (k + 1) * SSD_CHUNK) for k in range(n_chunks)]
        ends = [acs_ch[(k + 1) * SSD_CHUNK - 1:(k + 1) * SSD_CHUNK, :] for k in range(n_chunks)]
        grow, xdt_tail_b, chunk_decay = [], [], []
        for k, rk in enumerate(chunk_rows):
            start = ends[k - 1] if k else None
            grow.append(jnp.exp(acs_ch[rk] - start if k else acs_ch[rk]))
            xdt_tail_b.append((xdt[rk] * jnp.exp(ends[k] - acs_ch[rk])).astype(BF16))
            chunk_decay.append(jnp.exp(ends[k] - start if k else ends[k]))
        issue()

        y_rows = [[None] * (D_MODEL // LANES) for _ in range(n_chunks)]
        for grp in range(SSD_N_GROUPS):
            gs = slice(grp * SSD_GROUP_WIDTH, (grp + 1) * SSD_GROUP_WIDTH)
            state = state_ref[grp]
            for k, rk in enumerate(chunk_rows):
                bm_k = bmat[rk, grp * SSD_D_STATE:(grp + 1) * SSD_D_STATE]
                cm_k = cmat[rk, grp * SSD_D_STATE:(grp + 1) * SSD_D_STATE]
                cb = _dot_nt(cm_k, bm_k)
                y_off = _dot(cm_k, state.astype(BF16)) * grow[k][:, gs]
                state = state * chunk_decay[k][:, gs] + _dot_tn(bm_k, xdt_tail_b[k][:, gs])
                for pp in range(SSD_GROUP_WIDTH // LANES):
                    p = grp * (SSD_GROUP_WIDTH // LANES) + pp
                    blk = xdt_b[rk, p * LANES:(p + 1) * LANES]
                    m_pair = []
                    for h in (2 * p, 2 * p + 1):
                        seg = acs_cols[h][rk] - acs_t[h:h + 1, rk]
                        lmat = jnp.exp(jnp.where(causal, seg, -jnp.inf))
                        m_pair.append((cb * lmat).astype(BF16))
                    zero = jnp.zeros_like(blk)
                    x_pair = jnp.concatenate([jnp.where(lane_lo, blk, zero), jnp.where(lane_lo, zero, blk)], axis=0)
                    y_rows[k][p] = (y_off[:, pp * LANES:(pp + 1) * LANES]
                                    + _dot(jnp.concatenate(m_pair, axis=1), x_pair))
                issue()
            state_ref[grp] = state
        y = jnp.concatenate([jnp.concatenate(r, axis=1) for r in y_rows], axis=0) + xs * dskip_ref[...]
        z = pb[:, B_Z:B_Q].astype(F32)
        y = y * (z * jax.nn.sigmoid(z))
        y_norm = []
        for grp in range(SSD_N_GROUPS):
            yg = y[:, grp * SSD_GROUP_WIDTH:(grp + 1) * SSD_GROUP_WIDTH]
            y_norm.append(yg * lax.rsqrt(jnp.mean(yg * yg, axis=-1, keepdims=True) + EPS))
        y_ssd = (jnp.concatenate(y_norm, axis=1) * ssdnw_ref[...]).astype(BF16)
        issue()

        q = pb[:, B_Q:B_K] * (ATT_HEAD_DIM ** -0.5)
        k_buf[ATT_PAD:ATT_PAD + T, :] = pb[:, B_K:B_V]
        v_buf[ATT_PAD:ATT_PAD + T, :] = pb[:, B_V:PB_COLS]
        key_i = lax.broadcasted_iota(jnp.int32, (1, ATT_BAND), 1)
        rep = ATT_N_HEADS // ATT_N_KV
        sink_rows = jnp.concatenate([jnp.full((ATT_CHUNK, LANES), sink_ref[hq], F32)
                                     for hq in range(ATT_N_HEADS)], axis=0)
        att_rows = []
        for c in range(T // ATT_CHUNK):
            r0 = c * ATT_CHUNK
            kband = k_buf[r0:r0 + ATT_BAND, :]
            vband = v_buf[r0:r0 + ATT_BAND, :]
            q_stack = []
            for hq in range(ATT_N_HEADS):
                qblk = q[r0:r0 + ATT_CHUNK, (hq % rep) * LANES:(hq % rep + 1) * LANES]
                keep = lane_lo if hq < rep else jnp.logical_not(lane_lo)
                q_stack.append(jnp.where(keep, qblk, jnp.zeros_like(qblk)))
            s = _dot_nt(jnp.concatenate(q_stack, axis=0), kband) + bias_ref[...]
            if r0 < ATT_PAD:
                s = jnp.where((key_i + (r0 - ATT_PAD) + t0) >= 0, s, -jnp.inf)
            m = jnp.maximum(jnp.max(s, axis=-1, keepdims=True), sink_rows)
            pr = jnp.exp(s - jnp.concatenate([m, m[:, :ATT_BAND - LANES]], axis=1))
            den = jnp.sum(pr, axis=-1, keepdims=True) + jnp.exp(sink_rows - m)
            o = _dot(pr.astype(BF16), vband) * (1.0 / den)
            att_rows.append(jnp.concatenate(
                [jnp.where(lane_lo, o[jb * ATT_CHUNK:(jb + 1) * ATT_CHUNK],
                           o[(jb + rep) * ATT_CHUNK:(jb + rep + 1) * ATT_CHUNK]) for jb in range(rep)], axis=1))
            issue()
        y_att = jnp.concatenate(att_rows, axis=0).astype(BF16)
        k_buf[0:ATT_PAD, :] = k_buf[T:T + ATT_PAD, :]
        v_buf[0:ATT_PAD, :] = v_buf[T:T + ATT_PAD, :]

        tpos = t0 + lax.broadcasted_iota(jnp.int32, (T, 1), 0)
        pool_tiles = []
        for gi, w in enumerate(POOL_WINDOWS):
            ug = pf[:, F_U + gi * POOL_GROUP:F_U + (gi + 1) * POOL_GROUP]
            u_buf[gi, POOL_HIST:POOL_HIST + T, :] = ug
            acc = ug
            for back in range(1, w):
                acc = acc + u_buf[gi, POOL_HIST - back:POOL_HIST - back + T, :]
            u_buf[gi, 0:POOL_HIST, :] = u_buf[gi, T:T + POOL_HIST, :]
            cnt = jnp.minimum(tpos + 1, w).astype(F32)
            pooled = acc / cnt - ug
            pool_tiles.append(_dot(pooled.astype(BF16), poolw_ref[gi]))
            issue()
        y_pool = (jnp.concatenate(pool_tiles, axis=1) * pools_ref[...]).astype(BF16)
        assert points_left[0] == 0 and not ahead

        acc = []

        def branch(i, y_branch, w_ref):
            gate = jax.nn.sigmoid(pb[:, B_GATE + i * D_MODEL:B_GATE + (i + 1) * D_MODEL].astype(F32))
            acc.append(gate * _dot(y_branch, w_ref[...]))

        def finish():
            merged = acc[0] + acc[1] + acc[2]
            o_ref[rows, :] = _dot(merged.astype(BF16), wout_ref[...])

        return [functools.partial(branch, 0, y_ssd, wus_ref), functools.partial(branch, 1, y_att, wua_ref),
                functools.partial(branch, 2, y_pool, wup_ref), finish]

    merge_first = mix(pb0, pf0, first, tile_in_row * T, projection_steps(h2_ref, second, pb1, pf1))
    proj_next = projection_steps(hn_ref, slice(None), pb0, pf0)
    n_gate = sum(1 for a, _ in PB_SEGMENTS if a < B_Z)
    per_gate = n_gate // 3
    park_gate, park_rest, park_f = proj_next[0:n_gate], proj_next[n_gate:len(PB_SEGMENTS)], proj_next[len(PB_SEGMENTS):]
    ahead = list(park_f)
    for i in range(3):
        ahead += merge_first[i:i + 1] + park_gate[i * per_gate:(i + 1) * per_gate]
    ahead += park_rest + merge_first[3:4]
    for step in mix(pb1, pf1, second, (tile_in_row + 1) * T, ahead):
        step()


def _mlp_kernel(x_ref, d_ref, n2_ref, w1_ref, w2_ref, nnext_ref, o_ref, *maybe_hn_ref, last):
    x = x_ref[...] + d_ref[...]
    hb = _rmsnorm(x, n2_ref[...]).astype(BF16)
    a = jnp.maximum(_dot(hb, w1_ref[...]), 0.0)
    out = x + _dot((a * a).astype(BF16), w2_ref[...])
    if last:
        o_ref[...] = _rmsnorm(out, nnext_ref[...])
    else:
        o_ref[...] = out
        maybe_hn_ref[0][...] = _rmsnorm(out, nnext_ref[...]).astype(BF16)


def _norm_kernel(x_ref, w_ref, o_ref):
    o_ref[...] = _rmsnorm(x_ref[...], w_ref[...]).astype(BF16)


def _resident(shape, index):
    return pl.BlockSpec(shape, index, pipeline_mode=pl.Buffered(1))


def _mixer_call(xn, layer, p, tile):
    b, s, _ = xn.shape
    tiles_per_row = s // tile
    assert s % tile == 0 and tiles_per_row % 2 == 0 and tile % LANES == 0 and tile >= ATT_PAD
    n_tiles = b * tiles_per_row
    l3 = lambda g: (layer, 0, 0)
    l4 = lambda g: (layer, 0, 0, 0)
    vec = lambda n: _resident((None, 1, n), l3)
    in_specs = [
        pl.BlockSpec((None, 2 * tile, D_MODEL), lambda g: (g, 0, 0)),
        pl.BlockSpec((None, tile, D_MODEL), lambda g: (jnp.minimum(2 * g + 2, n_tiles - 1), 0, 0)),
        _resident((None, D_MODEL, C_END), l3),
        _resident((None, SSD_CONV, SSD_CONV_DIM), l3),
        vec(SSD_CONV_DIM),
        vec(LANES),
        vec(LANES),
        vec(D_MODEL),
        vec(D_MODEL),
        pl.BlockSpec(memory_space=pltpu.SMEM),
        _resident((ATT_N_HEADS * ATT_CHUNK, ATT_BAND), lambda g: (0, 0)),
        _resident((None, len(POOL_WINDOWS), POOL_GROUP, POOL_GROUP), l4),
        vec(POOL_WIDTH),
        _resident((None, D_MODEL, D_MODEL), l3),
        _resident((None, ATT_N_HEADS * ATT_HEAD_DIM, D_MODEL), l3),
        _resident((None, POOL_WIDTH, D_MODEL), l3),
        _resident((None, D_MODEL, D_MODEL), l3),
    ]
    out = pl.pallas_call(
        functools.partial(_mixer_kernel, tile=tile, tiles_per_row=tiles_per_row),
        grid=(n_tiles // 2,),
        in_specs=in_specs,
        out_specs=pl.BlockSpec((None, 2 * tile, D_MODEL), lambda g: (g, 0, 0)),
        out_shape=jax.ShapeDtypeStruct((n_tiles // 2, 2 * tile, D_MODEL), F32),
        scratch_shapes=[
            pltpu.VMEM((tile, PB_COLS), BF16), pltpu.VMEM((tile, PF_COLS), F32),
            pltpu.VMEM((tile, PB_COLS), BF16), pltpu.VMEM((tile, PF_COLS), F32),
            pltpu.VMEM((SSD_CONV_DIM // LANES, CONV_HIST + tile, LANES), F32),
            pltpu.VMEM((ATT_PAD + tile, LANES), BF16),
            pltpu.VMEM((ATT_PAD + tile, LANES), BF16),
            pltpu.VMEM((POOL_WIDTH // LANES, POOL_HIST + tile, LANES), F32),
            pltpu.VMEM((SSD_N_GROUPS, SSD_D_STATE, SSD_GROUP_WIDTH), F32),
        ],
        compiler_params=pltpu.CompilerParams(
            dimension_semantics=("arbitrary",), vmem_limit_bytes=VMEM_LIMIT),
        name="mixer",
    )(xn.reshape(n_tiles // 2, 2 * tile, D_MODEL),
      xn.reshape(n_tiles, tile, D_MODEL), p["w_in"], p["conv_w"], p["conv_b"], p["dt_bias"], p["a_log"], p["d_skip"],
      p["ssd_norm_w"], p["sinks"][layer], p["att_bias"], p["pool_w"], p["pool_scale"],
      p["w_up_ssd"], p["w_up_attn"], p["w_up_pool"], p["w_out"])
    return out.reshape(b, s, D_MODEL)


def _mlp_call(x2d, d2d, layer, p, tile, last):
    n, _ = x2d.shape
    assert n % tile == 0
    rows = pl.BlockSpec((tile, D_MODEL), lambda i: (i, 0))
    if last:
        next_norm = _resident((1, D_MODEL), lambda i: (0, 0))
        next_w, out_specs, out_shape = p["final_norm_w"], rows, jax.ShapeDtypeStruct(x2d.shape, F32)
    else:
        next_norm = _resident((None, 1, D_MODEL), lambda i: (layer + 1, 0, 0))
        next_w, out_specs = p["norm1_w"], (rows, rows)
        out_shape = (jax.ShapeDtypeStruct(x2d.shape, F32), jax.ShapeDtypeStruct(x2d.shape, BF16))
    l3 = lambda i: (layer, 0, 0)
    return pl.pallas_call(
        functools.partial(_mlp_kernel, last=last),
        grid=(n // tile,),
        in_specs=[rows, rows, _resident((None, 1, D_MODEL), l3), _resident((None, D_MODEL, D_FF), l3),
                  _resident((None, D_FF, D_MODEL), l3), next_norm],
        out_specs=out_specs,
        out_shape=out_shape,
        compiler_params=pltpu.CompilerParams(
            dimension_semantics=("arbitrary",), vmem_limit_bytes=VMEM_LIMIT),
        name="mlp",
    )(x2d, d2d, p["norm2_w"], p["w_mlp_in"], p["w_mlp_out"], next_w)


def _norm_call(x2d, p, tile):
    n, _ = x2d.shape
    rows = pl.BlockSpec((tile, D_MODEL), lambda i: (i, 0))
    return pl.pallas_call(
        _norm_kernel, grid=(n // tile,),
        in_specs=[rows, _resident((None, 1, D_MODEL), lambda i: (0, 0, 0))],
        out_specs=rows, out_shape=jax.ShapeDtypeStruct(x2d.shape, BF16), name="norm",
    )(x2d, p["norm1_w"])


def _prepare_params(norm1_w, w_in, ssd_conv_w, ssd_conv_b, ssd_dt_bias, ssd_a_log, ssd_d, ssd_norm_w,
                    attn_sinks, rel_bias, pool_w, pool_scale, w_up_ssd, w_up_attn, w_up_pool, w_out,
                    norm2_w, w_mlp_in, w_mlp_out, final_norm_w):
    depth = w_in.shape[0]
    w_in_b = w_in.astype(BF16)
    seg = lambda a, b: w_in_b[:, :, a:b]
    pad_lanes = lambda a: jnp.pad(a, ((0, 0),) * (a.ndim - 1) + ((0, LANES - a.shape[-1]),))
    w_in_k = jnp.concatenate([
        seg(0, _GATE_END), seg(_GATE_END, _Z_END),
        _pair_heads(seg(_DT_END, _Q_END), 2), seg(_Q_END, _K_END), seg(_K_END, _V_END),
        seg(_Z_END, _XBC_END), seg(_V_END, _IN_COLS), pad_lanes(seg(_XBC_END, _DT_END))],
        axis=-1)
    assert w_in_k.shape[-1] == C_END
    row = lambda a: a.astype(F32)[:, None, :]
    att_bias = jnp.moveaxis(rel_bias.astype(F32)[_t5_bucket_table()], -1, 0)
    att_bias = att_bias.reshape(ATT_N_HEADS * ATT_CHUNK, ATT_BAND)
    return {
        "norm1_w": row(norm1_w), "w_in": w_in_k,
        "conv_w": ssd_conv_w.astype(F32), "conv_b": row(ssd_conv_b),
        "dt_bias": row(pad_lanes(ssd_dt_bias)), "a_log": row(pad_lanes(ssd_a_log)),
        "d_skip": row(jnp.repeat(ssd_d, SSD_HEAD_DIM, axis=-1)), "ssd_norm_w": row(ssd_norm_w),
        "sinks": attn_sinks.astype(F32), "att_bias": att_bias,
        "pool_w": pool_w.astype(BF16), "pool_scale": row(pool_scale),
        "w_up_ssd": w_up_ssd.astype(BF16), "w_up_attn": _pair_heads(w_up_attn, 1).astype(BF16),
        "w_up_pool": w_up_pool.astype(BF16), "w_out": w_out.astype(BF16),
        "norm2_w": row(norm2_w), "w_mlp_in": w_mlp_in.astype(BF16), "w_mlp_out": w_mlp_out.astype(BF16),
        "final_norm_w": final_norm_w.astype(F32)[None, :], "depth": depth,
    }


def kernel(x, norm1_w, w_in, ssd_conv_w, ssd_conv_b, ssd_dt_bias, ssd_a_log, ssd_d, ssd_norm_w, attn_sinks, rel_bias, pool_w, pool_scale, w_up_ssd, w_up_attn, w_up_pool, w_out, norm2_w, w_mlp_in, w_mlp_out, final_norm_w):
    p = _prepare_params(norm1_w, w_in, ssd_conv_w, ssd_conv_b, ssd_dt_bias, ssd_a_log, ssd_d, ssd_norm_w,
                        attn_sinks, rel_bias, pool_w, pool_scale, w_up_ssd, w_up_attn, w_up_pool, w_out,
                        norm2_w, w_mlp_in, w_mlp_out, final_norm_w)
    b, s, d = x.shape
    seq_tile = min(SEQ_TILE, s // 2)
    mlp_tile = min(MLP_TILE, b * s)
    depth = p.pop("depth")
    x = x.astype(F32).reshape(b * s, d)
    xn = _norm_call(x, p, min(NORM_TILE, b * s))
    for layer in range(depth):
        update = _mixer_call(xn.reshape(b, s, d), layer, p, seq_tile).reshape(b * s, d)
        if layer == depth - 1:
            return _mlp_call(x, update, layer, p, mlp_tile, True).reshape(b, s, d)
        x, xn = _mlp_call(x, update, layer, p, mlp_tile, False)
```
